```python
import math
import jax, jax.numpy as jnp
from jax import lax
import numpy as np

D_MODEL = 1024
BATCH = 2
SEQ = 8192
DEPTH = 2
DEC_BATCH = 128
DEC_SEQ = 8
PAST_LEN = 2048
PAGE_SIZE = 128

N_A_LAYERS = DEPTH // 2
N_B_LAYERS = DEPTH - N_A_LAYERS
D_FF = 4 * D_MODEL
EPS = 1e-6

A_HEADS = 8
A_DK = D_MODEL // A_HEADS
A_DV = D_MODEL // A_HEADS
A_QK_W = A_HEADS * A_DK
A_V_W = A_HEADS * A_DV
A_CONV = 4
A_CHUNK = 64
A_CONV_W = 2 * A_QK_W + A_V_W
A_IN_W = A_CONV_W + A_V_W + 2 * A_HEADS

B_HEADS = 8
B_HD = D_MODEL // (2 * B_HEADS)
B_QK_W = B_HEADS * 2 * B_HD
B_V_W = B_HEADS * 2 * B_HD
ROPE_THETA = 500000.0
ROT_DIM = B_HD // 4
Q_BLOCK = 128

kernel_name = "yoco_gdn_diffattn_step"


def rmsnorm(x, g):
    xf = x.astype(jnp.float32)
    y = xf * lax.rsqrt(jnp.mean(xf * xf, axis=-1, keepdims=True) + EPS) * g.astype(jnp.float32)
    return y.astype(x.dtype)


def l2norm(x):
    xf = x.astype(jnp.float32)
    return xf * lax.rsqrt(jnp.sum(xf * xf, axis=-1, keepdims=True) + EPS)


def rope(x, pos):
    half = ROT_DIM // 2
    inv = ROPE_THETA ** (-jnp.arange(0, ROT_DIM, 2, dtype=jnp.float32) / ROT_DIM)
    ang = pos.astype(jnp.float32)[:, None] * inv[None, :]
    cos = jnp.cos(ang)[:, None, None, :]
    sin = jnp.sin(ang)[:, None, None, :]
    xf = x.astype(jnp.float32)
    x1, x2 = xf[..., :half], xf[..., half:ROT_DIM]
    out = jnp.concatenate([x1 * cos - x2 * sin, x2 * cos + x1 * sin, xf[..., ROT_DIM:]], axis=-1)
    return out.astype(x.dtype)


def mlp(h, ln, w_up, w_down):
    u = rmsnorm(h, ln) @ w_up
    return jnp.square(jax.nn.relu(u)) @ w_down


def causal_conv(seq_in, buf, conv_w):
    length = seq_in.shape[1]
    seq = jnp.concatenate([buf.astype(seq_in.dtype), seq_in], axis=1)
    out = sum(seq[:, w:w + length] * conv_w[w] for w in range(A_CONV))
    return out, seq[:, length:]


def gated_delta_rule(q, k, v, g, beta, s0, chunk):
    bsz, length, nh, dk = q.shape
    n = length // chunk

    def to_chunks(t):
        t = t.reshape(bsz, n, chunk, nh, *t.shape[3:])
        return jnp.moveaxis(jnp.moveaxis(t, 1, 0), 3, 2)

    qc, kc, vc, gc, bc = (to_chunks(t) for t in (q, k, v, g, beta))
    cg = jnp.cumsum(gc, axis=-1)
    idx = jnp.arange(chunk)
    incl = idx[:, None] >= idx[None, :]
    strict = idx[:, None] > idx[None, :]
    diff = cg[..., :, None] - cg[..., None, :]
    decay = jnp.where(incl, jnp.exp(jnp.where(incl, diff, 0.0)), 0.0)
    kk = jnp.einsum("nbhcd,nbhed->nbhce", kc, kc)
    lower = jnp.where(strict, decay * kk, 0.0) * bc[..., :, None]
    m = jnp.eye(chunk, dtype=q.dtype) + lower
    rhs = jnp.concatenate([(bc * jnp.exp(cg))[..., None] * kc, bc[..., None] * vc], axis=-1)
    sol = lax.linalg.triangular_solve(m, rhs, left_side=True, lower=True, unit_diagonal=True)
    w_mat, u_base = sol[..., :dk], sol[..., dk:]
    attn = decay * jnp.einsum("nbhcd,nbhed->nbhce", qc, kc)
    q_dec = qc * jnp.exp(cg)[..., None]
    k_dec = kc * jnp.exp(cg[..., -1:] - cg)[..., None]
    g_end = jnp.exp(cg[..., -1])

    def step(s, xs):
        w_c, u_c, a_c, q_c, k_c, e_c = xs
        u = u_c - jnp.einsum("bhcd,bhdv->bhcv", w_c, s)
        o = jnp.einsum("bhcd,bhdv->bhcv", q_c, s) + jnp.einsum("bhce,bhev->bhcv", a_c, u)
        s = s * e_c[..., None, None] + jnp.einsum("bhcd,bhcv->bhdv", k_c, u)
        return s, o

    s_fin, o = lax.scan(step, s0, (w_mat, u_base, attn, q_dec, k_dec, g_end))
    o = jnp.moveaxis(jnp.moveaxis(o, 2, 3), 0, 1).reshape(bsz, length, nh, v.shape[-1])
    return o, s_fin


def delta_mixer(x, s0, conv_buf, w_in, conv_w, a_log, dt_bias, o_norm, w_out, chunk):
    bsz, length, _ = x.shape
    proj = x @ w_in
    qkv, conv_new = causal_conv(proj[..., :A_CONV_W], conv_buf, conv_w)
    qkv = jax.nn.silu(qkv)
    z = proj[..., A_CONV_W:A_CONV_W + A_V_W].reshape(bsz, length, A_HEADS, A_DV)
    b = proj[..., A_CONV_W + A_V_W:A_CONV_W + A_V_W + A_HEADS]
    a = proj[..., A_CONV_W + A_V_W + A_HEADS:]
    q = l2norm(qkv[..., :A_QK_W].reshape(bsz, length, A_HEADS, A_DK)) * (A_DK ** -0.5)
    k = l2norm(qkv[..., A_QK_W:2 * A_QK_W].reshape(bsz, length, A_HEADS, A_DK))
    v = qkv[..., 2 * A_QK_W:].reshape(bsz, length, A_HEADS, A_DV).astype(jnp.float32)
    beta = jax.nn.sigmoid(b.astype(jnp.float32))
    g = -jnp.exp(a_log.astype(jnp.float32)) * jax.nn.softplus(a.astype(jnp.float32) + dt_bias.astype(jnp.float32))
    o, s_fin = gated_delta_rule(q, k, v, g, beta, s0.astype(jnp.float32), chunk)
    o = rmsnorm(o, o_norm) * jax.nn.silu(z.astype(jnp.float32))
    y = o.reshape(bsz, length, A_V_W).astype(x.dtype) @ w_out
    return y, s_fin.astype(x.dtype), conv_new


def shared_kv(h, ln_kv, w_kv, k_norm, pos):
    bsz, length, _ = h.shape
    kv = rmsnorm(h, ln_kv) @ w_kv
    k = kv[..., :B_QK_W].reshape(bsz, length, B_HEADS, 2, B_HD)
    v = kv[..., B_QK_W:].reshape(bsz, length, B_HEADS, 2 * B_HD)
    return rope(rmsnorm(k, k_norm), pos), v


def diff_attend(q, keys, values, mask, lam):
    s = jnp.concatenate([jnp.einsum("bqhmd,bkhmd->bhmqk", q, k, preferred_element_type=jnp.float32)
                         for k in keys], axis=-1) * (B_HD ** -0.5)
    s = jnp.where(mask, s, jnp.finfo(jnp.float32).min)
    p = jax.nn.softmax(s, axis=-1)
    a = p[:, :, 0] - lam * p[:, :, 1]
    bounds = [int(b) for b in np.cumsum([k.shape[1] for k in keys])[:-1]]
    parts = jnp.split(a, bounds, axis=-1)
    return sum(jnp.einsum("bhqk,bkhe->bqhe", a_i, v_i.astype(jnp.float32))
               for a_i, v_i in zip(parts, values))


def diff_mixer(x, keys, values, k_pos, q_pos, w_q, q_norm, lq1, lk1, lq2, lk2, subln, w_out, lam_init):
    bsz, length, _ = x.shape
    q = (x @ w_q).reshape(bsz, length, B_HEADS, 2, B_HD)
    q = rope(rmsnorm(q, q_norm), q_pos)
    f32 = jnp.float32
    lam = (jnp.exp(jnp.sum(lq1.astype(f32) * lk1.astype(f32)))
           - jnp.exp(jnp.sum(lq2.astype(f32) * lk2.astype(f32))) + lam_init)
    qb = math.gcd(length, Q_BLOCK)
    nb = length // qb
    q_blocks = jnp.moveaxis(q.reshape(bsz, nb, qb, B_HEADS, 2, B_HD), 1, 0)
    p_blocks = q_pos.reshape(nb, qb)

    def attend_block(args):
        q_blk, p_blk = args
        return diff_attend(q_blk, keys, values, k_pos[None, :] <= p_blk[:, None], lam)

    o = lax.map(attend_block, (q_blocks, p_blocks))
    o = jnp.moveaxis(o, 0, 1).reshape(bsz, length, B_HEADS, 2 * B_HD)
    o = rmsnorm(o, subln) * (1.0 - lam_init)
    return o.reshape(bsz, length, B_V_W).astype(x.dtype) @ w_out


def setup_inputs(seed: int = 0) -> dict:
    key = jax.random.key(seed)
    ks = iter(jax.random.split(key, 40))
    f32 = jnp.float32

    def nrm(shape, scale):
        return jax.random.normal(next(ks), shape, f32) * scale

    def gain(shape):
        return 1.0 + 0.02 * jax.random.normal(next(ks), shape, f32)

    n_pages = PAST_LEN // PAGE_SIZE
    n_used = DEC_BATCH * n_pages
    n_pool = n_used + n_used // 4
    page_table = jax.random.permutation(next(ks), n_pool)[:n_used].reshape(DEC_BATCH, n_pages).astype(jnp.int32)
    a_log = jnp.log(jax.random.uniform(next(ks), (N_A_LAYERS, A_HEADS), f32, 1.0, 16.0))
    dt = jnp.exp(jax.random.uniform(next(ks), (N_A_LAYERS, A_HEADS), f32, math.log(1e-3), math.log(1e-1)))
    dt_bias = dt + jnp.log(-jnp.expm1(-dt))
    return {
        "x_prompt": nrm((BATCH, SEQ, D_MODEL), 1.0),
        "x_sample": nrm((DEC_BATCH, DEC_SEQ, D_MODEL), 1.0),
        "cache_k": nrm((n_pool, PAGE_SIZE, B_HEADS, 2, B_HD), 1.0),
        "cache_v": nrm((n_pool, PAGE_SIZE, B_HEADS, 2 * B_HD), 1.0),
        "page_table": page_table,
        "state_delta": nrm((N_A_LAYERS, DEC_BATCH, A_HEADS, A_DK, A_DV), 0.3),
        "state_conv": nrm((N_A_LAYERS, DEC_BATCH, A_CONV - 1, A_CONV_W), 1.0),
        "ln_mix": gain((DEPTH, D_MODEL)),
        "ln_mlp": gain((DEPTH, D_MODEL)),
        "w_up": nrm((DEPTH, D_MODEL, D_FF), D_MODEL ** -0.5),
        "w_down": nrm((DEPTH, D_FF, D_MODEL), 0.5 * D_FF ** -0.5),
        "w_in_a": nrm((N_A_LAYERS, D_MODEL, A_IN_W), D_MODEL ** -0.5),
        "conv_a": nrm((N_A_LAYERS, A_CONV, A_CONV_W), A_CONV ** -0.5),
        "a_log": a_log,
        "dt_bias": dt_bias,
        "o_norm_a": gain((N_A_LAYERS, A_DV)),
        "w_out_a": nrm((N_A_LAYERS, A_V_W, D_MODEL), A_V_W ** -0.5),
        "ln_kv": gain((D_MODEL,)),
        "w_kv": nrm((D_MODEL, B_QK_W + B_V_W), D_MODEL ** -0.5),
        "k_norm": gain((B_HD,)),
        "w_q_b": nrm((N_B_LAYERS, D_MODEL, B_QK_W), D_MODEL ** -0.5),
        "q_norm_b": gain((N_B_LAYERS, B_HD)),
        "lam_q1": nrm((N_B_LAYERS, B_HD), 0.1),
        "lam_k1": nrm((N_B_LAYERS, B_HD), 0.1),
        "lam_q2": nrm((N_B_LAYERS, B_HD), 0.1),
        "lam_k2": nrm((N_B_LAYERS, B_HD), 0.1),
        "subln_b": gain((N_B_LAYERS, 2 * B_HD)),
        "w_out_b": nrm((N_B_LAYERS, B_V_W, D_MODEL), B_V_W ** -0.5),
    }


def reference(x_prompt, x_sample, cache_k, cache_v, page_table, state_delta, state_conv,
              ln_mix, ln_mlp, w_up, w_down, w_in_a, conv_a, a_log, dt_bias, o_norm_a, w_out_a,
              ln_kv, w_kv, k_norm, w_q_b, q_norm_b, lam_q1, lam_k1, lam_q2, lam_k2, subln_b, w_out_b):
    n_pages = PAST_LEN // PAGE_SIZE
    pos_p = jnp.arange(SEQ, dtype=jnp.int32)
    pos_s = PAST_LEN + jnp.arange(DEC_SEQ, dtype=jnp.int32)
    kpos_s = jnp.arange(PAST_LEN + DEC_SEQ, dtype=jnp.int32)
    hp, hs = x_prompt, x_sample
    sd_p, sd_s, sc_p, sc_s = [], [], [], []
    for layer in range(DEPTH):
        if layer < N_A_LAYERS:
            i = layer
            aw = (w_in_a[i], conv_a[i], a_log[i], dt_bias[i], o_norm_a[i], w_out_a[i])
            zs = jnp.zeros((BATCH, A_HEADS, A_DK, A_DV), x_prompt.dtype)
            zc = jnp.zeros((BATCH, A_CONV - 1, A_CONV_W), x_prompt.dtype)
            mp, s_new_p, c_new_p = delta_mixer(rmsnorm(hp, ln_mix[layer]), zs, zc, *aw, A_CHUNK)
            ms, s_new_s, c_new_s = delta_mixer(rmsnorm(hs, ln_mix[layer]), state_delta[i], state_conv[i],
                                               *aw, math.gcd(DEC_SEQ, A_CHUNK))
            sd_p.append(s_new_p)
            sd_s.append(s_new_s)
            sc_p.append(c_new_p)
            sc_s.append(c_new_s)
        else:
            if layer == N_A_LAYERS:
                k_p, v_p = shared_kv(hp, ln_kv, w_kv, k_norm, pos_p)
                k_s, v_s = shared_kv(hs, ln_kv, w_kv, k_norm, pos_s)
                past_k = cache_k[page_table].reshape(DEC_BATCH, n_pages * PAGE_SIZE, B_HEADS, 2, B_HD)
                past_v = cache_v[page_table].reshape(DEC_BATCH, n_pages * PAGE_SIZE, B_HEADS, 2 * B_HD)
            j = layer - N_A_LAYERS
            lam_init = 0.8 - 0.6 * math.exp(-0.3 * layer)
            bw = (w_q_b[j], q_norm_b[j], lam_q1[j], lam_k1[j], lam_q2[j], lam_k2[j], subln_b[j], w_out_b[j], lam_init)
            mp = diff_mixer(rmsnorm(hp, ln_mix[layer]), (k_p,), (v_p,), pos_p, pos_p, *bw)
            ms = diff_mixer(rmsnorm(hs, ln_mix[layer]), (past_k, k_s), (past_v, v_s), kpos_s, pos_s, *bw)
        hp = hp + mp
        hs = hs + ms
        hp = hp + mlp(hp, ln_mlp[layer], w_up[layer], w_down[layer])
        hs = hs + mlp(hs, ln_mlp[layer], w_up[layer], w_down[layer])
    return (hp, hs, k_p, v_p, k_s, v_s, jnp.stack(sd_p), jnp.stack(sd_s), jnp.stack(sc_p), jnp.stack(sc_s))
```

```python
import functools
import math

import jax
import jax.numpy as jnp
from jax import lax
from jax.experimental import pallas as pl
from jax.experimental.pallas import tpu as pltpu

F32 = jnp.float32
BF16 = jnp.bfloat16
EPS = 1e-6

D_MODEL = 1024
D_FF = 4 * D_MODEL
N_HEADS = 8
HEAD_W = 128
MAP_W = 64
CONV_TAPS = 4
CONV_W = 3 * D_MODEL
DELTA_CHUNK = 64
ROT_DIM = 16
ROPE_THETA = 500000.0
PAGE = 128
LANES = 128
SUBLANES = 8
VMEM_LIMIT = 56 * 1024 * 1024

_NT = (((1,), (1,)), ((), ()))


def _dot(a, b):
    return jnp.dot(a.astype(BF16), b.astype(BF16), preferred_element_type=F32)


def _dot_nt(a, b):
    return lax.dot_general(a.astype(BF16), b.astype(BF16), _NT, preferred_element_type=F32)


def _rms(x, g):
    return x * lax.rsqrt(jnp.mean(x * x, axis=-1, keepdims=True) + EPS) * g


def _sigmoid(x):
    return 1.0 / (1.0 + jnp.exp(-x))


def _softplus(x):
    return jnp.maximum(x, 0.0) + jnp.log1p(jnp.exp(-jnp.abs(x)))


def _const_spec(shape):
    nd = len(shape)
    return pl.BlockSpec(shape, lambda *_: (0,) * nd, pipeline_mode=pl.Buffered(1))


def _params(*sem):
    return pltpu.CompilerParams(dimension_semantics=sem, vmem_limit_bytes=VMEM_LIMIT)


def _row_tile(t, want):
    tm = min(t, want)
    assert t % tm == 0
    return tm


def _proj_in_kernel(x_ref, ln_ref, w_ref, wba_ref, qkv_ref, z_ref, ba_ref):
    h = _rms(x_ref[...], ln_ref[...]).astype(BF16)
    qkv_ref[...] = jnp.dot(h, w_ref[:, :CONV_W], preferred_element_type=F32)
    z_ref[...] = jnp.dot(h, w_ref[:, CONV_W:], preferred_element_type=F32)
    ba_ref[...] = jnp.dot(h, wba_ref[...], preferred_element_type=F32)


def _proj_in(x, ln, w_main, w_ba):
    t = x.shape[0]
    tm = _row_tile(t, 512)
    row = lambda w: pl.BlockSpec((tm, w), lambda i: (i, 0))
    return pl.pallas_call(
        _proj_in_kernel,
        grid=(t // tm,),
        in_specs=[row(D_MODEL), _const_spec((1, D_MODEL)), _const_spec(w_main.shape), _const_spec(w_ba.shape)],
        out_specs=[row(CONV_W), row(D_MODEL), row(2 * LANES)],
        out_shape=[jax.ShapeDtypeStruct((t, CONV_W), F32), jax.ShapeDtypeStruct((t, D_MODEL), F32),
                   jax.ShapeDtypeStruct((t, 2 * LANES), F32)],
        compiler_params=_params("parallel"),
        name="proj_in",
    )(x, ln, w_main, w_ba)


def _delta_kernel(*refs, chunk, n_chunks, has_init):
    if has_init:
        (qkv_ref, z_ref, ba_ref, s0_ref, cb_ref, cw_ref, alog_ref, dtb_ref, onorm_ref,
         o_ref, sfin_ref, ctail_ref, xp_ref, q_s, k_s, v_s, g_s, beta_s, s_ref) = refs
    else:
        (qkv_ref, z_ref, ba_ref, cw_ref, alog_ref, dtb_ref, onorm_ref,
         o_ref, sfin_ref, ctail_ref, xp_ref, q_s, k_s, v_s, g_s, beta_s, s_ref) = refs
    c = chunk
    blk = c * n_chunks
    i = pl.program_id(1)
    halo = SUBLANES

    @pl.when(i == 0)
    def _():
        if has_init:
            s_ref[...] = s0_ref[0]
            xp_ref[0:halo, :] = cb_ref[0]
        else:
            s_ref[...] = jnp.zeros_like(s_ref)
            xp_ref[0:halo, :] = jnp.zeros((halo, CONV_W), F32)

    xp_ref[halo:halo + blk, :] = qkv_ref[0]

    for j in range(CONV_W // LANES):
        cols = slice(j * LANES, (j + 1) * LANES)
        acc = None
        for w in range(CONV_TAPS):
            r0 = halo - (CONV_TAPS - 1) + w
            term = xp_ref[r0:r0 + blk, cols] * cw_ref[w:w + 1, cols]
            acc = term if acc is None else acc + term
        y = acc * _sigmoid(acc)
        if j < 2 * N_HEADS:
            y = y * lax.rsqrt(jnp.sum(y * y, axis=-1, keepdims=True) + EPS)
        if j < N_HEADS:
            q_s[:, cols] = y * (HEAD_W ** -0.5)
        elif j < 2 * N_HEADS:
            k_s[:, (j - N_HEADS) * LANES:(j - N_HEADS + 1) * LANES] = y
        else:
            v_s[:, (j - 2 * N_HEADS) * LANES:(j - 2 * N_HEADS + 1) * LANES] = y

    tail = xp_ref[blk:blk + halo, :]
    xp_ref[0:halo, :] = tail

    @pl.when(i == pl.num_programs(1) - 1)
    def _():
        ctail_ref[0] = tail

    ba = ba_ref[0]
    beta_s[...] = _sigmoid(ba[:, :LANES])
    g_s[...] = -jnp.exp(alog_ref[...]) * _softplus(ba[:, LANES:] + dtb_ref[...])

    row = lax.broadcasted_iota(jnp.int32, (c, c), 0)
    col = lax.broadcasted_iota(jnp.int32, (c, c), 1)
    incl = row >= col
    strict = row > col
    ltri = incl.astype(F32)
    eye_c = (row == col).astype(F32)
    lane = lax.broadcasted_iota(jnp.int32, (c, LANES), 1)
    r128 = lax.broadcasted_iota(jnp.int32, (LANES, LANES), 0)
    c128 = lax.broadcasted_iota(jnp.int32, (LANES, LANES), 1)
    eye_l = (r128 == c128).astype(BF16)
    onorm = onorm_ref[...]
    n_sq = int(math.log2(c)) - 1

    def chunk_body(ci, carry):
        r0 = ci * c if isinstance(ci, int) else pl.multiple_of(ci * c, c)
        rows = pl.ds(r0, c)
        g_c = g_s[rows, :]
        beta_c = beta_s[rows, :]
        cg = jnp.dot(ltri, g_c, preferred_element_type=F32, precision=lax.Precision.HIGHEST)
        cg_last = cg[c - 1:c, :]
        ecg = jnp.exp(cg)
        ekd = jnp.exp(cg_last - cg)
        eend = jnp.exp(cg_last)
        bec = beta_c * ecg
        cgr = pltpu.roll(cg, LANES // 2, axis=1)
        for h in range(N_HEADS):
            cols = slice(h * LANES, (h + 1) * LANES)
            q = q_s[rows, cols]
            k = k_s[rows, cols]
            v = v_s[rows, cols]
            g1 = jnp.where(lane == h, cg, jnp.where(lane == LANES // 2 + h, 1.0, 0.0))
            g2 = jnp.where(lane == h, 1.0, jnp.where(lane == LANES // 2 + h, -cgr, 0.0))
            diff = lax.dot_general(g1, g2, _NT, preferred_element_type=F32, precision=lax.Precision.HIGHEST)
            decay = jnp.where(incl, jnp.exp(jnp.where(incl, diff, 0.0)), 0.0)
            kb = k.astype(BF16)
            kk = _dot_nt(kb, kb)
            qk = _dot_nt(q, kb)
            beta_h = beta_c[:, h:h + 1]
            nmat = -(jnp.where(strict, decay * kk, 0.0) * beta_h)
            pmat = eye_c + nmat
            qmat = nmat
            for _ in range(n_sq):
                qmat = _dot(qmat, qmat)
                pmat = pmat + _dot(pmat, qmat)
            rhs = jnp.concatenate([k * bec[:, h:h + 1], v * beta_h], axis=1)
            sol = _dot(pmat, rhs)
            w_mat = sol[:, :LANES]
            u_base = sol[:, LANES:]
            attn = decay * qk
            q_dec = q * ecg[:, h:h + 1]
            k_dec = k * ekd[:, h:h + 1]
            kd_t = lax.dot_general(eye_l, k_dec.astype(BF16), _NT, preferred_element_type=F32)
            s_old = s_ref[h]
            r = _dot(jnp.concatenate([w_mat, q_dec], axis=0), s_old)
            u = u_base - r[:c]
            o = r[c:] + _dot(attn, u)
            s_ref[h] = s_old * eend[:, h:h + 1] + _dot(kd_t, u)
            on = _rms(o, onorm)
            zt = z_ref[0, rows, cols]
            o_ref[0, rows, cols] = (on * (zt * _sigmoid(zt))).astype(o_ref.dtype)
        return carry

    if n_chunks == 1:
        chunk_body(0, 0)
    else:
        lax.fori_loop(0, n_chunks, chunk_body, 0)

    @pl.when(i == pl.num_programs(1) - 1)
    def _():
        sfin_ref[0] = s_ref[...]


def _delta(qkv, z, ba, conv_w, a_log, dt_bias, o_norm, *, chunk, n_chunks, s0=None, conv_buf=None, out_dtype=BF16):
    bsz, length, _ = qkv.shape
    blk = chunk * n_chunks
    assert length % blk == 0 and blk >= SUBLANES
    has_init = s0 is not None
    seq = lambda w: pl.BlockSpec((1, blk, w), lambda b, i: (b, i, 0))
    in_specs = [seq(CONV_W), seq(D_MODEL), seq(2 * LANES)]
    args = [qkv, z, ba]
    if has_init:
        in_specs += [pl.BlockSpec((1, N_HEADS, HEAD_W, HEAD_W), lambda b, i: (b, 0, 0, 0)),
                     pl.BlockSpec((1, SUBLANES, CONV_W), lambda b, i: (b, 0, 0))]
        args += [s0, conv_buf]
    in_specs += [_const_spec(conv_w.shape), _const_spec((1, LANES)), _const_spec((1, LANES)), _const_spec((1, HEAD_W))]
    args += [conv_w, a_log, dt_bias, o_norm]
    return pl.pallas_call(
        functools.partial(_delta_kernel, chunk=chunk, n_chunks=n_chunks, has_init=has_init),
        grid=(bsz, length // blk),
        in_specs=in_specs,
        out_specs=[seq(D_MODEL),
                   pl.BlockSpec((1, N_HEADS, HEAD_W, HEAD_W), lambda b, i: (b, 0, 0, 0)),
                   pl.BlockSpec((1, SUBLANES, CONV_W), lambda b, i: (b, 0, 0))],
        out_shape=[jax.ShapeDtypeStruct((bsz, length, D_MODEL), out_dtype),
                   jax.ShapeDtypeStruct((bsz, N_HEADS, HEAD_W, HEAD_W), F32),
                   jax.ShapeDtypeStruct((bsz, SUBLANES, CONV_W), F32)],
        scratch_shapes=[pltpu.VMEM((blk + SUBLANES, CONV_W), F32),
                        pltpu.VMEM((blk, D_MODEL), F32), pltpu.VMEM((blk, D_MODEL), F32), pltpu.VMEM((blk, D_MODEL), F32),
                        pltpu.VMEM((blk, LANES), F32), pltpu.VMEM((blk, LANES), F32),
                        pltpu.VMEM((N_HEADS, HEAD_W, HEAD_W), F32)],
        compiler_params=_params("parallel", "arbitrary"),
        name="delta_rule",
    )(*args)


def _post_kernel(x_ref, o_ref, wo_ref, ln_ref, wup_ref, wdn_ref, y_ref, *, ff_chunk):
    h1 = x_ref[...] + jnp.dot(o_ref[...].astype(BF16), wo_ref[...], preferred_element_type=F32)
    n = _rms(h1, ln_ref[...]).astype(BF16)
    acc = h1
    for c0 in range(0, D_FF, ff_chunk):
        u = jnp.dot(n, wup_ref[:, c0:c0 + ff_chunk], preferred_element_type=F32)
        a = jnp.square(jnp.maximum(u, 0.0)).astype(BF16)
        acc = acc + jnp.dot(a, wdn_ref[c0:c0 + ff_chunk, :], preferred_element_type=F32)
    y_ref[...] = acc


def _post(x, o, w_out, ln, w_up, w_down):
    t = x.shape[0]
    tm = _row_tile(t, 512)
    row = pl.BlockSpec((tm, D_MODEL), lambda i: (i, 0))
    return pl.pallas_call(
        functools.partial(_post_kernel, ff_chunk=1024),
        grid=(t // tm,),
        in_specs=[row, row, _const_spec(w_out.shape), _const_spec((1, D_MODEL)),
                  _const_spec(w_up.shape), _const_spec(w_down.shape)],
        out_specs=row,
        out_shape=jax.ShapeDtypeStruct((t, D_MODEL), F32),
        compiler_params=_params("parallel"),
        name="post_mlp",
    )(x, o, w_out, ln, w_up, w_down)


def _head_norm_rope(x, gain, gmat, cos, sin_a, sin_b):
    outs = []
    for j in range(D_MODEL // LANES):
        xt = x[:, j * LANES:(j + 1) * LANES]
        ms = jnp.dot(xt * xt, gmat, preferred_element_type=F32, precision=lax.Precision.HIGHEST)
        xn = xt * lax.rsqrt(ms + EPS) * gain
        outs.append(xn * cos + pltpu.roll(xn, LANES - ROT_DIM // 2, axis=1) * sin_a
                    + pltpu.roll(xn, ROT_DIM // 2, axis=1) * sin_b)
    return jnp.concatenate(outs, axis=1)


def _proj_kvq_kernel(h_ref, lnkv_ref, lnq_ref, wkv_ref, wq_ref, kn_ref, qn_ref, gmat_ref, cos_ref, sa_ref, sb_ref,
                     k_ref, v_ref, kb_ref, vb_ref, qb_ref):
    h = h_ref[...]
    cos, sin_a, sin_b, gmat = cos_ref[...], sa_ref[...], sb_ref[...], gmat_ref[...]
    nkv = _rms(h, lnkv_ref[...]).astype(BF16)
    k = jnp.dot(nkv, wkv_ref[:, :D_MODEL], preferred_element_type=F32)
    v = jnp.dot(nkv, wkv_ref[:, D_MODEL:], preferred_element_type=F32)
    k = _head_norm_rope(k, kn_ref[...], gmat, cos, sin_a, sin_b)
    k_ref[...] = k
    v_ref[...] = v
    kb_ref[...] = k.astype(BF16)
    vb_ref[...] = v.astype(BF16)
    nq = _rms(h, lnq_ref[...]).astype(BF16)
    q = jnp.dot(nq, wq_ref[...], preferred_element_type=F32)
    q = _head_norm_rope(q, qn_ref[...], gmat, cos, sin_a, sin_b)
    qb_ref[...] = (q * (MAP_W ** -0.5)).astype(BF16)


def _proj_kvq(h, ln_kv, ln_q, w_kv, w_q, k_norm, q_norm, gmat, cos, sin_a, sin_b):
    t = h.shape[0]
    tm = _row_tile(t, min(512, cos.shape[0]))
    n_tab = cos.shape[0] // tm
    row = pl.BlockSpec((tm, D_MODEL), lambda i: (i, 0))
    tab = pl.BlockSpec((tm, LANES), lambda i: (i % n_tab, 0))
    f32_out = jax.ShapeDtypeStruct((t, D_MODEL), F32)
    bf_out = jax.ShapeDtypeStruct((t, D_MODEL), BF16)
    return pl.pallas_call(
        _proj_kvq_kernel,
        grid=(t // tm,),
        in_specs=[row, _const_spec((1, D_MODEL)), _const_spec((1, D_MODEL)), _const_spec(w_kv.shape), _const_spec(w_q.shape),
                  _const_spec((1, LANES)), _const_spec((1, LANES)), _const_spec((LANES, LANES)), tab, tab, tab],
        out_specs=[row] * 5,
        out_shape=[f32_out, f32_out, bf_out, bf_out, bf_out],
        compiler_params=_params("parallel"),
        name="proj_kvq",
    )(h, ln_kv, ln_q, w_kv, w_q, k_norm, q_norm, gmat, cos, sin_a, sin_b)


def _lam_from(lamv_ref, lam_init):
    lv = lamv_ref[...]
    s1 = jnp.sum(lv[0:1, :] * lv[1:2, :], axis=-1, keepdims=True)
    s2 = jnp.sum(lv[2:3, :] * lv[3:4, :], axis=-1, keepdims=True)
    return jnp.exp(s1) - jnp.exp(s2) + lam_init


def _attn_prompt_kernel(q_ref, k_ref, v_ref, lamv_ref, subln_ref, o_ref, m_ref, l_ref, acc_ref, *, bq, lam_init):
    i = pl.program_id(2)
    q = q_ref[0]
    lane = lax.broadcasted_iota(jnp.int32, (bq, LANES), 1)
    zero = jnp.zeros_like(q)
    q2 = jnp.concatenate([jnp.where(lane < MAP_W, q, zero), jnp.where(lane >= MAP_W, q, zero)], axis=0)
    m_ref[...] = jnp.full(m_ref.shape, -jnp.inf, F32)
    l_ref[...] = jnp.zeros_like(l_ref)
    acc_ref[...] = jnp.zeros_like(acc_ref)

    def step(j, masked):
        r0 = pl.multiple_of(j * bq, bq)
        kb = k_ref[0, pl.ds(r0, bq), :]
        vb = v_ref[0, pl.ds(r0, bq), :]
        s = lax.dot_general(q2, kb, _NT, preferred_element_type=F32)
        if masked:
            qi = lax.broadcasted_iota(jnp.int32, (2 * bq, bq), 0) % bq
            ki = lax.broadcasted_iota(jnp.int32, (2 * bq, bq), 1)
            s = jnp.where(ki <= qi, s, -jnp.inf)
        m_old = m_ref[...]
        m_new = jnp.maximum(m_old, jnp.max(s, axis=-1, keepdims=True))
        alpha = jnp.exp(m_old - m_new)
        p = jnp.exp(s - m_new)
        l_ref[...] = alpha * l_ref[...] + jnp.sum(p, axis=-1, keepdims=True)
        acc_ref[...] = alpha * acc_ref[...] + jnp.dot(p.astype(BF16), vb, preferred_element_type=F32)
        m_ref[...] = m_new

    def body(j, carry):
        step(j, False)
        return carry

    lax.fori_loop(0, i, body, 0)
    step(i, True)

    lam = _lam_from(lamv_ref, lam_init)
    o_all = acc_ref[...] / l_ref[...]
    o = o_all[:bq] - lam * o_all[bq:]
    o_ref[0] = (_rms(o, subln_ref[...]) * (1.0 - lam_init)).astype(o_ref.dtype)


def _attn_prompt(q, k, v, lamv, subln, lam_init):
    bsz, length, _ = q.shape
    bq = min(512, length)
    assert length % bq == 0
    return pl.pallas_call(
        functools.partial(_attn_prompt_kernel, bq=bq, lam_init=lam_init),
        grid=(bsz, N_HEADS, length // bq),
        in_specs=[pl.BlockSpec((1, bq, HEAD_W), lambda b, h, i: (b, i, h)),
                  pl.BlockSpec((1, length, HEAD_W), lambda b, h, i: (b, 0, h)),
                  pl.BlockSpec((1, length, HEAD_W), lambda b, h, i: (b, 0, h)),
                  _const_spec(lamv.shape), _const_spec((1, HEAD_W))],
        out_specs=pl.BlockSpec((1, bq, HEAD_W), lambda b, h, i: (b, i, h)),
        out_shape=jax.ShapeDtypeStruct((bsz, length, D_MODEL), BF16),
        scratch_shapes=[pltpu.VMEM((2 * bq, 1), F32), pltpu.VMEM((2 * bq, 1), F32), pltpu.VMEM((2 * bq, HEAD_W), F32)],
        compiler_params=_params("parallel", "parallel", "arbitrary"),
        name="attn_prompt",
    )(q, k, v, lamv, subln)


def _attn_sample_kernel(pt_ref, q_ref, kc_ref, vc_ref, kn_ref, vn_ref, lamv_ref, subln_ref, o_ref,
                        qbd_ref, m_ref, l_ref, acc_ref, *, n_q, lam_init):
    del pt_ref
    p_idx = pl.program_id(1)
    n_pages = pl.num_programs(1) - 1
    grp = 2 * n_q

    @pl.when(p_idx == 0)
    def _():
        q = q_ref[0]
        lane = lax.broadcasted_iota(jnp.int32, (n_q, D_MODEL), 1) // MAP_W
        zero = jnp.zeros_like(q)
        for hm in range(2 * N_HEADS):
            qbd_ref[hm * n_q:(hm + 1) * n_q, :] = jnp.where(lane == hm, q, zero)
        m_ref[...] = jnp.full(m_ref.shape, -jnp.inf, F32)
        l_ref[...] = jnp.zeros_like(l_ref)
        acc_ref[...] = jnp.zeros_like(acc_ref)

    def update(kpage, vpage, mask):
        s = jnp.concatenate(
            [lax.dot_general(qbd_ref[h * grp:(h + 1) * grp, h * HEAD_W:(h + 1) * HEAD_W],
                             kpage[:, h * HEAD_W:(h + 1) * HEAD_W], _NT, preferred_element_type=F32)
             for h in range(N_HEADS)], axis=0)
        if mask is not None:
            s = jnp.where(mask, s, -jnp.inf)
        m_old = m_ref[...]
        m_new = jnp.maximum(m_old, jnp.max(s, axis=-1, keepdims=True))
        alpha = jnp.exp(m_old - m_new)
        p = jnp.exp(s - m_new)
        l_ref[...] = alpha * l_ref[...] + jnp.sum(p, axis=-1, keepdims=True)
        pb = p.astype(BF16)
        pv = jnp.concatenate(
            [jnp.dot(pb[h * grp:(h + 1) * grp, :], vpage[:, h * HEAD_W:(h + 1) * HEAD_W], preferred_element_type=F32)
             for h in range(N_HEADS)], axis=0)
        acc_ref[...] = alpha * acc_ref[...] + pv
        m_ref[...] = m_new

    @pl.when(p_idx < n_pages)
    def _():
        update(kc_ref[0].astype(BF16), vc_ref[0].astype(BF16), None)

    @pl.when(p_idx == n_pages)
    def _():
        pad = jnp.zeros((PAGE - n_q, D_MODEL), BF16)
        kpage = jnp.concatenate([kn_ref[0], pad], axis=0)
        vpage = jnp.concatenate([vn_ref[0], pad], axis=0)
        r = lax.broadcasted_iota(jnp.int32, (N_HEADS * grp, PAGE), 0) % n_q
        kidx = lax.broadcasted_iota(jnp.int32, (N_HEADS * grp, PAGE), 1)
        update(kpage, vpage, kidx <= r)
        lam = _lam_from(lamv_ref, lam_init)
        o_all = acc_ref[...] / l_ref[...]
        for h in range(N_HEADS):
            o = o_all[h * grp:h * grp + n_q] - lam * o_all[h * grp + n_q:(h + 1) * grp]
            o_ref[0, :, h * HEAD_W:(h + 1) * HEAD_W] = (_rms(o, subln_ref[...]) * (1.0 - lam_init)).astype(o_ref.dtype)


def _attn_sample(q, cache_k, cache_v, page_table, k_new, v_new, lamv, subln, lam_init):
    bsz, n_q, _ = q.shape
    n_pages = page_table.shape[1]
    rows = 2 * N_HEADS * n_q
    per_b = lambda b, p, pt: (b, 0, 0)
    page = lambda b, p, pt: (pt[b, jnp.minimum(p, n_pages - 1)], 0, 0)
    grid_spec = pltpu.PrefetchScalarGridSpec(
        num_scalar_prefetch=1,
        grid=(bsz, n_pages + 1),
        in_specs=[pl.BlockSpec((1, n_q, D_MODEL), per_b),
                  pl.BlockSpec((1, PAGE, D_MODEL), page), pl.BlockSpec((1, PAGE, D_MODEL), page),
                  pl.BlockSpec((1, n_q, D_MODEL), per_b), pl.BlockSpec((1, n_q, D_MODEL), per_b),
                  pl.BlockSpec(lamv.shape, lambda b, p, pt: (0, 0)), pl.BlockSpec((1, HEAD_W), lambda b, p, pt: (0, 0))],
        out_specs=pl.BlockSpec((1, n_q, D_MODEL), per_b),
        scratch_shapes=[pltpu.VMEM((rows, D_MODEL), BF16), pltpu.VMEM((rows, 1), F32), pltpu.VMEM((rows, 1), F32),
                        pltpu.VMEM((rows, HEAD_W), F32)],
    )
    return pl.pallas_call(
        functools.partial(_attn_sample_kernel, n_q=n_q, lam_init=lam_init),
        grid_spec=grid_spec,
        out_shape=jax.ShapeDtypeStruct((bsz, n_q, D_MODEL), F32),
        compiler_params=_params("parallel", "arbitrary"),
        name="attn_sample",
    )(page_table, q, cache_k, cache_v, k_new, v_new, lamv, subln)


def _rope_tables(pos):
    half = ROT_DIM // 2
    inv = ROPE_THETA ** (-jnp.arange(0, ROT_DIM, 2, dtype=F32) / ROT_DIM)
    ang = pos.astype(F32)[:, None] * inv[None, :]
    cos, sin = jnp.cos(ang), jnp.sin(ang)
    n = pos.shape[0]
    ones = jnp.ones((n, MAP_W - ROT_DIM), F32)
    zeros = jnp.zeros((n, MAP_W - ROT_DIM), F32)
    z8 = jnp.zeros((n, half), F32)
    cos_g = jnp.concatenate([cos, cos, ones], axis=1)
    sa_g = jnp.concatenate([-sin, z8, zeros], axis=1)
    sb_g = jnp.concatenate([z8, sin, zeros], axis=1)
    rep = LANES // MAP_W
    return jnp.tile(cos_g, (1, rep)), jnp.tile(sa_g, (1, rep)), jnp.tile(sb_g, (1, rep))


def _pad_lanes(v):
    return jnp.zeros((1, LANES), F32).at[0, :v.shape[0]].set(v.astype(F32))


def kernel(x_prompt, x_sample, cache_k, cache_v, page_table, state_delta, state_conv, ln_mix, ln_mlp, w_up, w_down,
           w_in_a, conv_a, a_log, dt_bias, o_norm_a, w_out_a, ln_kv, w_kv, k_norm, w_q_b, q_norm_b,
           lam_q1, lam_k1, lam_q2, lam_k2, subln_b, w_out_b):
    bsz, seq, _ = x_prompt.shape
    dbsz, dseq, _ = x_sample.shape
    n_pages = page_table.shape[1]
    past_len = n_pages * PAGE
    tp, ts = bsz * seq, dbsz * dseq
    row2 = lambda v: v.reshape(1, -1).astype(F32)

    w_in = w_in_a[0]
    w_main = w_in[:, :CONV_W + D_MODEL].astype(BF16)
    w_ba = jnp.zeros((D_MODEL, 2 * LANES), F32)
    w_ba = w_ba.at[:, :N_HEADS].set(w_in[:, CONV_W + D_MODEL:CONV_W + D_MODEL + N_HEADS])
    w_ba = w_ba.at[:, LANES:LANES + N_HEADS].set(w_in[:, CONV_W + D_MODEL + N_HEADS:]).astype(BF16)
    ln0 = row2(ln_mix[0])
    alog, dtb, onorm = _pad_lanes(a_log[0]), _pad_lanes(dt_bias[0]), row2(o_norm_a[0])

    xp2, xs2 = x_prompt.reshape(tp, D_MODEL), x_sample.reshape(ts, D_MODEL)
    qkv_p, z_p, ba_p = _proj_in(xp2, ln0, w_main, w_ba)
    qkv_s, z_s, ba_s = _proj_in(xs2, ln0, w_main, w_ba)

    nchunk_p = min(4, seq // DELTA_CHUNK)
    o_p, sd_p, ct_p = _delta(qkv_p.reshape(bsz, seq, CONV_W), z_p.reshape(bsz, seq, D_MODEL),
                             ba_p.reshape(bsz, seq, 2 * LANES), conv_a[0], alog, dtb, onorm,
                             chunk=DELTA_CHUNK, n_chunks=nchunk_p)
    chunk_s = math.gcd(dseq, DELTA_CHUNK)
    cbuf = jnp.concatenate([jnp.zeros((dbsz, SUBLANES - (CONV_TAPS - 1), CONV_W), F32), state_conv[0]], axis=1)
    o_s, sd_s, ct_s = _delta(qkv_s.reshape(dbsz, dseq, CONV_W), z_s.reshape(dbsz, dseq, D_MODEL),
                             ba_s.reshape(dbsz, dseq, 2 * LANES), conv_a[0], alog, dtb, onorm,
                             chunk=chunk_s, n_chunks=dseq // chunk_s, s0=state_delta[0], conv_buf=cbuf, out_dtype=F32)

    wo_a, wup0, wdn0 = w_out_a[0].astype(BF16), w_up[0].astype(BF16), w_down[0].astype(BF16)
    h_p = _post(xp2, o_p.reshape(tp, D_MODEL), wo_a, row2(ln_mlp[0]), wup0, wdn0)
    h_s = _post(xs2, o_s.reshape(ts, D_MODEL), wo_a, row2(ln_mlp[0]), wup0, wdn0)

    lam_init = 0.8 - 0.6 * math.exp(-0.3 * 1)
    wkv, wq = w_kv.astype(BF16), w_q_b[0].astype(BF16)
    kn = jnp.tile(k_norm.astype(F32), LANES // MAP_W).reshape(1, LANES)
    qn = jnp.tile(q_norm_b[0].astype(F32), LANES // MAP_W).reshape(1, LANES)
    gi = jnp.arange(LANES) // MAP_W
    gmat = (gi[:, None] == gi[None, :]).astype(F32) / MAP_W
    cos_p, sa_p, sb_p = _rope_tables(jnp.arange(seq, dtype=jnp.int32))
    pos_s = past_len + jnp.arange(dseq, dtype=jnp.int32)
    tile_s = min(512, ts) // dseq
    cos_s, sa_s, sb_s = (jnp.tile(t, (tile_s, 1)) for t in _rope_tables(pos_s))
    ln1 = row2(ln_mix[1])
    k_p, v_p, kb_p, vb_p, qb_p = _proj_kvq(h_p, row2(ln_kv), ln1, wkv, wq, kn, qn, gmat, cos_p, sa_p, sb_p)
    k_s, v_s, kb_s, vb_s, qb_s = _proj_kvq(h_s, row2(ln_kv), ln1, wkv, wq, kn, qn, gmat, cos_s, sa_s, sb_s)

    lamv = jnp.zeros((SUBLANES, LANES), F32)
    for r, vec in enumerate((lam_q1[0], lam_k1[0], lam_q2[0], lam_k2[0])):
        lamv = lamv.at[r, :MAP_W].set(vec.astype(F32))
    subln = row2(subln_b[0])
    a_p = _attn_prompt(qb_p.reshape(bsz, seq, D_MODEL), kb_p.reshape(bsz, seq, D_MODEL),
                       vb_p.reshape(bsz, seq, D_MODEL), lamv, subln, lam_init)
    n_pool = cache_k.shape[0]
    a_s = _attn_sample(qb_s.reshape(dbsz, dseq, D_MODEL), cache_k.reshape(n_pool, PAGE, D_MODEL),
                       cache_v.reshape(n_pool, PAGE, D_MODEL), page_table,
                       kb_s.reshape(dbsz, dseq, D_MODEL), vb_s.reshape(dbsz, dseq, D_MODEL), lamv, subln, lam_init)

    wo_b, wup1, wdn1 = w_out_b[0].astype(BF16), w_up[1].astype(BF16), w_down[1].astype(BF16)
    y_p = _post(h_p, a_p.reshape(tp, D_MODEL), wo_b, row2(ln_mlp[1]), wup1, wdn1)
    y_s = _post(h_s, a_s.reshape(ts, D_MODEL), wo_b, row2(ln_mlp[1]), wup1, wdn1)

    tail = slice(SUBLANES - (CONV_TAPS - 1), SUBLANES)
    return (y_p.reshape(bsz, seq, D_MODEL), y_s.reshape(dbsz, dseq, D_MODEL),
            k_p.reshape(bsz, seq, N_HEADS, 2, MAP_W), v_p.reshape(bsz, seq, N_HEADS, HEAD_W),
            k_s.reshape(dbsz, dseq, N_HEADS, 2, MAP_W), v_s.reshape(dbsz, dseq, N_HEADS, HEAD_W),
            sd_p[None], sd_s[None], ct_p[:, tail][None], ct_s[:, tail][None])
```

```python
import functools
import math

import jax
import jax.numpy as jnp
from jax import lax
from jax.experimental import pallas as pl
from jax.experimental.pallas import tpu as pltpu

F32 = jnp.float32
BF16 = jnp.bfloat16
EPS = 1e-6

D_MODEL = 1024
D_FF = 4 * D_MODEL
N_HEADS = 8
HEAD_W = 128
MAP_W = 64
CONV_TAPS = 4
CONV_W = 3 * D_MODEL
DELTA_CHUNK = 64
ROT_DIM = 16
ROPE_THETA = 500000.0
PAGE = 128
LANES = 128
SUBLANES = 8
VMEM_LIMIT = 56 * 1024 * 1024

_NT = (((1,), (1,)), ((), ()))
_HI = lax.Precision.HIGHEST


def _dot(a, b):
    return jnp.dot(a.astype(BF16), b.astype(BF16), preferred_element_type=F32)


def _dot_nt(a, b):
    return lax.dot_general(a.astype(BF16), b.astype(BF16), _NT, preferred_element_type=F32)


def _rms(x, g):
    return x * lax.rsqrt(jnp.mean(x * x, axis=-1, keepdims=True) + EPS) * g


def _sigmoid(x):
    return 1.0 / (1.0 + jnp.exp(-x))


def _softplus(x):
    return jnp.maximum(x, 0.0) + jnp.log1p(jnp.exp(-jnp.abs(x)))


def _eye(n, dtype):
    return (lax.broadcasted_iota(jnp.int32, (n, n), 0) == lax.broadcasted_iota(jnp.int32, (n, n), 1)).astype(dtype)


def _const_spec(shape):
    nd = len(shape)
    return pl.BlockSpec(shape, lambda *_: (0,) * nd, pipeline_mode=pl.Buffered(1))


def _params(*sem):
    return pltpu.CompilerParams(dimension_semantics=sem, vmem_limit_bytes=VMEM_LIMIT)


def _row_tile(t, want):
    tm = min(t, want)
    assert t % tm == 0
    return tm


def _proj_in_kernel(x_ref, ln_ref, w_ref, wba_ref, qkv_ref, z_ref, ba_ref):
    h = _rms(x_ref[...], ln_ref[...]).astype(BF16)
    qkv_ref[...] = jnp.dot(h, w_ref[:, :CONV_W], preferred_element_type=F32)
    z_ref[...] = jnp.dot(h, w_ref[:, CONV_W:], preferred_element_type=F32)
    ba_ref[...] = jnp.dot(h, wba_ref[...], preferred_element_type=F32)


def _proj_in(x, ln, w_main, w_ba):
    t = x.shape[0]
    tm = _row_tile(t, 512)
    row = lambda w: pl.BlockSpec((tm, w), lambda i: (i, 0))
    return pl.pallas_call(
        _proj_in_kernel,
        grid=(t // tm,),
        in_specs=[row(D_MODEL), _const_spec((1, D_MODEL)), _const_spec(w_main.shape), _const_spec(w_ba.shape)],
        out_specs=[row(CONV_W), row(D_MODEL), row(2 * LANES)],
        out_shape=[jax.ShapeDtypeStruct((t, CONV_W), F32), jax.ShapeDtypeStruct((t, D_MODEL), F32),
                   jax.ShapeDtypeStruct((t, 2 * LANES), F32)],
        compiler_params=_params("parallel"),
        name="proj_in",
    )(x, ln, w_main, w_ba)


def _delta_kernel(*refs, chunk, n_chunks, bb, group, has_init):
    if has_init:
        (qkv_ref, z_ref, ba_ref, s0_ref, cb_ref, cw_ref, alog_ref, dtb_ref, onorm_ref,
         o_ref, sfin_ref, ctail_ref,
         xp_ref, q_s, k_s, v_s, g_s, beta_s, s_ref, wq_s, ub_s, at_s, kd_s, ee_s) = refs
    else:
        (qkv_ref, z_ref, ba_ref, cw_ref, alog_ref, dtb_ref, onorm_ref,
         o_ref, sfin_ref, ctail_ref,
         xp_ref, q_s, k_s, v_s, g_s, beta_s, s_ref, wq_s, ub_s, at_s, kd_s, ee_s) = refs
    c = chunk
    blk = c * n_chunks
    i = pl.program_id(1)
    last = pl.num_programs(1) - 1
    halo = SUBLANES

    @pl.when(i == 0)
    def _():
        if has_init:
            s_ref[...] = s0_ref[...]
            xp_ref[:, 0:halo, :] = cb_ref[...]
        else:
            s_ref[...] = jnp.zeros_like(s_ref)
            xp_ref[:, 0:halo, :] = jnp.zeros((bb, halo, CONV_W), F32)

    for bi in range(bb):
        xp_ref[bi, halo:halo + blk, :] = qkv_ref[bi]
        for j in range(CONV_W // LANES):
            cols = slice(j * LANES, (j + 1) * LANES)
            acc = None
            for w in range(CONV_TAPS):
                r0 = halo - (CONV_TAPS - 1) + w
                term = xp_ref[bi, r0:r0 + blk, cols] * cw_ref[w:w + 1, cols]
                acc = term if acc is None else acc + term
            y = acc * _sigmoid(acc)
            if j < 2 * N_HEADS:
                y = y * lax.rsqrt(jnp.sum(y * y, axis=-1, keepdims=True) + EPS)
            if j < N_HEADS:
                q_s[bi, :, cols] = y * (HEAD_W ** -0.5)
            elif j < 2 * N_HEADS:
                k_s[bi, :, (j - N_HEADS) * LANES:(j - N_HEADS + 1) * LANES] = y
            else:
                v_s[bi, :, (j - 2 * N_HEADS) * LANES:(j - 2 * N_HEADS + 1) * LANES] = y
        tail = xp_ref[bi, blk:blk + halo, :]
        xp_ref[bi, 0:halo, :] = tail

        @pl.when(i == last)
        def _():
            ctail_ref[bi] = tail

        ba = ba_ref[bi]
        beta_s[bi] = _sigmoid(ba[:, :LANES])
        g_s[bi] = -jnp.exp(alog_ref[...]) * _softplus(ba[:, LANES:] + dtb_ref[...])

    row = lax.broadcasted_iota(jnp.int32, (c, c), 0)
    col = lax.broadcasted_iota(jnp.int32, (c, c), 1)
    incl = row >= col
    strict = row > col
    ltri = incl.astype(F32)
    utri = (row <= col).astype(F32)
    eye_c = (row == col).astype(F32)
    eye_f = _eye(LANES, F32)
    eye_b = eye_f.astype(BF16)
    onorm = onorm_ref[...]
    n_sq = int(math.log2(c)) - 1
    heads = range(N_HEADS)
    hcols = [slice(h * LANES, (h + 1) * LANES) for h in heads]

    def phase_a(groups):
        inst = [(gi, h) for gi in range(len(groups)) for h in heads]
        rows, cg, cg_t, beta_c, ecg, ekd, bec = [], [], [], [], [], [], []
        for bi, ci in groups:
            r0 = ci * c if isinstance(ci, int) else pl.multiple_of(ci * c, c)
            rs = pl.ds(r0, c)
            g_c = g_s[bi, rs, :]
            b_c = beta_s[bi, rs, :]
            cg_c = jnp.dot(ltri, g_c, preferred_element_type=F32, precision=_HI)
            g_t = lax.dot_general(eye_f, g_c, _NT, preferred_element_type=F32, precision=_HI)
            cg_t.append(jnp.dot(g_t, utri, preferred_element_type=F32, precision=_HI))
            cg_last = cg_c[c - 1:c, :]
            e_c = jnp.exp(cg_c)
            ee_s[bi, ci] = jnp.exp(cg_last)
            rows.append(rs)
            cg.append(cg_c)
            beta_c.append(b_c)
            ecg.append(e_c)
            ekd.append(jnp.exp(cg_last - cg_c))
            bec.append(b_c * e_c)
        q = [q_s[groups[gi][0], rows[gi], hcols[h]] for gi, h in inst]
        k = [k_s[groups[gi][0], rows[gi], hcols[h]] for gi, h in inst]
        v = [v_s[groups[gi][0], rows[gi], hcols[h]] for gi, h in inst]
        kb = [x.astype(BF16) for x in k]
        kk = [_dot_nt(x, x) for x in kb]
        qk = [_dot_nt(a, b) for a, b in zip(q, kb)]
        beta_h = [beta_c[gi][:, h:h + 1] for gi, h in inst]
        diff = [cg[gi][:, h:h + 1] - cg_t[gi][h:h + 1, :] for gi, h in inst]
        decay = [jnp.where(incl, jnp.exp(jnp.where(incl, d, 0.0)), 0.0) for d in diff]
        nmat = [-(jnp.where(strict, dc * x, 0.0) * b) for dc, x, b in zip(decay, kk, beta_h)]
        pmat = [eye_c + n for n in nmat]
        qmat = [_dot(n, n) for n in nmat]
        for it in range(n_sq):
            if it < n_sq - 1:
                pq = [_dot(jnp.concatenate([p, n], axis=0), n) for p, n in zip(pmat, qmat)]
                pmat = [p + x[:c] for p, x in zip(pmat, pq)]
                qmat = [x[c:] for x in pq]
            else:
                pmat = [p + _dot(p, n) for p, n in zip(pmat, qmat)]
        rhs = [jnp.concatenate([kx * bec[gi][:, h:h + 1], vx * b], axis=1)
               for (gi, h), kx, vx, b in zip(inst, k, v, beta_h)]
        sol = [_dot(p, r) for p, r in zip(pmat, rhs)]
        k_dec = [kx * ekd[gi][:, h:h + 1] for (gi, h), kx in zip(inst, k)]
        kd_t = [lax.dot_general(eye_b, x.astype(BF16), _NT, preferred_element_type=F32) for x in k_dec]
        for n, (gi, h) in enumerate(inst):
            bi, ci = groups[gi]
            q_dec = q[n] * ecg[gi][:, h:h + 1]
            wq_s[bi, ci, h] = jnp.concatenate([sol[n][:, :LANES], q_dec], axis=0).astype(BF16)
            ub_s[bi, ci, h] = sol[n][:, LANES:]
            at_s[bi, ci, h] = decay[n] * qk[n]
            kd_s[bi, ci, h] = kd_t[n]

    def phase_b(ci):
        inst = [(bi, h) for bi in range(bb) for h in heads]
        r0 = ci * c if isinstance(ci, int) else pl.multiple_of(ci * c, c)
        rs = pl.ds(r0, c)
        s_old = [s_ref[bi, h] for bi, h in inst]
        r = [jnp.dot(wq_s[bi, ci, h], s.astype(BF16), preferred_element_type=F32) for (bi, h), s in zip(inst, s_old)]
        u = [ub_s[bi, ci, h] - x[:c] for (bi, h), x in zip(inst, r)]
        ub16 = [x.astype(BF16) for x in u]
        o = [x[c:] + _dot(at_s[bi, ci, h], y) for (bi, h), x, y in zip(inst, r, ub16)]
        for (bi, h), s, y in zip(inst, s_old, ub16):
            s_ref[bi, h] = s * ee_s[bi, ci][:, h:h + 1] + _dot(kd_s[bi, ci, h], y)
        for (bi, h), x in zip(inst, o):
            zt = z_ref[bi, rs, hcols[h]]
            o_ref[bi, rs, hcols[h]] = (_rms(x, onorm) * (zt * _sigmoid(zt))).astype(o_ref.dtype)

    if n_chunks == 1:
        phase_a([(bi, 0) for bi in range(bb)])
        phase_b(0)
    else:
        def a_body(gi, carry):
            phase_a([(bi, gi * group + t) for bi in range(bb) for t in range(group)])
            return carry

        def b_body(ci, carry):
            phase_b(ci)
            return carry

        lax.fori_loop(0, n_chunks // group, a_body, 0)
        lax.fori_loop(0, n_chunks, b_body, 0)

    @pl.when(i == last)
    def _():
        sfin_ref[...] = s_ref[...]


def _delta(qkv, z, ba, conv_w, a_log, dt_bias, o_norm, *, chunk, n_chunks, bb=1, group=1,
           s0=None, conv_buf=None, out_dtype=BF16):
    bsz, length, _ = qkv.shape
    blk = chunk * n_chunks
    assert length % blk == 0 and blk >= SUBLANES and bsz % bb == 0 and n_chunks % group == 0
    has_init = s0 is not None
    seq = lambda w: pl.BlockSpec((bb, blk, w), lambda b, i: (b, i, 0))
    state = pl.BlockSpec((bb, N_HEADS, HEAD_W, HEAD_W), lambda b, i: (b, 0, 0, 0))
    ctail = pl.BlockSpec((bb, SUBLANES, CONV_W), lambda b, i: (b, 0, 0))
    in_specs = [seq(CONV_W), seq(D_MODEL), seq(2 * LANES)]
    args = [qkv, z, ba]
    if has_init:
        in_specs += [state, ctail]
        args += [s0, conv_buf]
    in_specs += [_const_spec(conv_w.shape), _const_spec((1, LANES)), _const_spec((1, LANES)), _const_spec((1, HEAD_W))]
    args += [conv_w, a_log, dt_bias, o_norm]
    per = (bb, n_chunks, N_HEADS)
    return pl.pallas_call(
        functools.partial(_delta_kernel, chunk=chunk, n_chunks=n_chunks, bb=bb, group=group, has_init=has_init),
        grid=(bsz // bb, length // blk),
        in_specs=in_specs,
        out_specs=[seq(D_MODEL), state, ctail],
        out_shape=[jax.ShapeDtypeStruct((bsz, length, D_MODEL), out_dtype),
                   jax.ShapeDtypeStruct((bsz, N_HEADS, HEAD_W, HEAD_W), F32),
                   jax.ShapeDtypeStruct((bsz, SUBLANES, CONV_W), F32)],
        scratch_shapes=[pltpu.VMEM((bb, blk + SUBLANES, CONV_W), F32),
                        pltpu.VMEM((bb, blk, D_MODEL), F32), pltpu.VMEM((bb, blk, D_MODEL), F32),
                        pltpu.VMEM((bb, blk, D_MODEL), F32),
                        pltpu.VMEM((bb, blk, LANES), F32), pltpu.VMEM((bb, blk, LANES), F32),
                        pltpu.VMEM((bb, N_HEADS, HEAD_W, HEAD_W), F32),
                        pltpu.VMEM(per + (2 * chunk, HEAD_W), BF16), pltpu.VMEM(per + (chunk, HEAD_W), F32),
                        pltpu.VMEM(per + (chunk, chunk), F32), pltpu.VMEM(per + (HEAD_W, chunk), F32),
                        pltpu.VMEM((bb, n_chunks, 1, LANES), F32)],
        compiler_params=_params("parallel", "arbitrary"),
        name="delta_rule",
    )(*args)


def _post_kernel(x_ref, o_ref, wo_ref, ln_ref, wup_ref, wdn_ref, y_ref, *, ff_chunk):
    h1 = x_ref[...] + jnp.dot(o_ref[...].astype(BF16), wo_ref[...], preferred_element_type=F32)
    n = _rms(h1, ln_ref[...]).astype(BF16)
    acc = h1
    for c0 in range(0, D_FF, ff_chunk):
        u = jnp.dot(n, wup_ref[:, c0:c0 + ff_chunk], preferred_element_type=F32)
        a = jnp.square(jnp.maximum(u, 0.0)).astype(BF16)
        acc = acc + jnp.dot(a, wdn_ref[c0:c0 + ff_chunk, :], preferred_element_type=F32)
    y_ref[...] = acc


def _post(x, o, w_out, ln, w_up, w_down):
    t = x.shape[0]
    tm = _row_tile(t, 512)
    row = pl.BlockSpec((tm, D_MODEL), lambda i: (i, 0))
    return pl.pallas_call(
        functools.partial(_post_kernel, ff_chunk=1024),
        grid=(t // tm,),
        in_specs=[row, row, _const_spec(w_out.shape), _const_spec((1, D_MODEL)),
                  _const_spec(w_up.shape), _const_spec(w_down.shape)],
        out_specs=row,
        out_shape=jax.ShapeDtypeStruct((t, D_MODEL), F32),
        compiler_params=_params("parallel"),
        name="post_mlp",
    )(x, o, w_out, ln, w_up, w_down)


def _head_norm_rope(x, gain, gmat, cos, sin_a, sin_b):
    outs = []
    for j in range(D_MODEL // LANES):
        xt = x[:, j * LANES:(j + 1) * LANES]
        ms = jnp.dot(xt * xt, gmat, preferred_element_type=F32, precision=_HI)
        xn = xt * lax.rsqrt(ms + EPS) * gain
        outs.append(xn * cos + pltpu.roll(xn, LANES - ROT_DIM // 2, axis=1) * sin_a
                    + pltpu.roll(xn, ROT_DIM // 2, axis=1) * sin_b)
    return jnp.concatenate(outs, axis=1)


def _proj_kvq_kernel(h_ref, lnkv_ref, lnq_ref, wkv_ref, wq_ref, kn_ref, qn_ref, gmat_ref, cos_ref, sa_ref, sb_ref,
                     k_ref, v_ref, kb_ref, vb_ref, qb_ref):
    h = h_ref[...]
    cos, sin_a, sin_b, gmat = cos_ref[...], sa_ref[...], sb_ref[...], gmat_ref[...]
    nkv = _rms(h, lnkv_ref[...]).astype(BF16)
    k = jnp.dot(nkv, wkv_ref[:, :D_MODEL], preferred_element_type=F32)
    v = jnp.dot(nkv, wkv_ref[:, D_MODEL:], preferred_element_type=F32)
    k = _head_norm_rope(k, kn_ref[...], gmat, cos, sin_a, sin_b)
    k_ref[...] = k
    v_ref[...] = v
    kb_ref[...] = k.astype(BF16)
    vb_ref[...] = v.astype(BF16)
    nq = _rms(h, lnq_ref[...]).astype(BF16)
    q = jnp.dot(nq, wq_ref[...], preferred_element_type=F32)
    q = _head_norm_rope(q, qn_ref[...], gmat, cos, sin_a, sin_b)
    qb_ref[...] = (q * (MAP_W ** -0.5)).astype(BF16)


def _proj_kvq(h, ln_kv, ln_q, w_kv, w_q, k_norm, q_norm, gmat, cos, sin_a, sin_b):
    t = h.shape[0]
    tm = _row_tile(t, min(512, cos.shape[0]))
    n_tab = cos.shape[0] // tm
    row = pl.BlockSpec((tm, D_MODEL), lambda i: (i, 0))
    tab = pl.BlockSpec((tm, LANES), lambda i: (i % n_tab, 0))
    f32_out = jax.ShapeDtypeStruct((t, D_MODEL), F32)
    bf_out = jax.ShapeDtypeStruct((t, D_MODEL), BF16)
    return pl.pallas_call(
        _proj_kvq_kernel,
        grid=(t // tm,),
        in_specs=[row, _const_spec((1, D_MODEL)), _const_spec((1, D_MODEL)), _const_spec(w_kv.shape), _const_spec(w_q.shape),
                  _const_spec((1, LANES)), _const_spec((1, LANES)), _const_spec((LANES, LANES)), tab, tab, tab],
        out_specs=[row] * 5,
        out_shape=[f32_out, f32_out, bf_out, bf_out, bf_out],
        compiler_params=_params("parallel"),
        name="proj_kvq",
    )(h, ln_kv, ln_q, w_kv, w_q, k_norm, q_norm, gmat, cos, sin_a, sin_b)


def _norm_rope_t(x, gain, cos, sin):
    half = ROT_DIM // 2
    outs = []
    for g in range(HEAD_W // MAP_W):
        xg = x[g * MAP_W:(g + 1) * MAP_W, :]
        ms = jnp.mean(xg * xg, axis=0, keepdims=True)
        xn = xg * lax.rsqrt(ms + EPS) * gain[g * MAP_W:(g + 1) * MAP_W, :]
        x1, x2 = xn[0:half], xn[half:ROT_DIM]
        outs += [x1 * cos - x2 * sin, x2 * cos + x1 * sin, xn[ROT_DIM:]]
    return jnp.concatenate(outs, axis=0)


def _proj_kvq_t_kernel(h_ref, lnkv_ref, lnq_ref, wkvt_ref, wv_ref, wqt_ref, kn_ref, qn_ref, cos_ref, sin_ref,
                       kt_ref, v_ref, kb_ref, vt_ref, qt_ref):
    h = h_ref[...]
    tm = h.shape[0]
    hn = h * lax.rsqrt(jnp.mean(h * h, axis=-1, keepdims=True) + EPS)
    nkv = (hn * lnkv_ref[...]).astype(BF16)
    nq = (hn * lnq_ref[...]).astype(BF16)
    cos, sin = cos_ref[...], sin_ref[...]
    rep = tm // LANES
    kn = jnp.concatenate([kn_ref[...]] * rep, axis=1)
    qn = jnp.concatenate([qn_ref[...]] * rep, axis=1)
    kvt = lax.dot_general(wkvt_ref[...], nkv, _NT, preferred_element_type=F32)
    kt = jnp.concatenate([_norm_rope_t(kvt[hd * HEAD_W:(hd + 1) * HEAD_W, :], kn, cos, sin)
                          for hd in range(N_HEADS)], axis=0)
    kt_ref[0] = kt
    vt_ref[0] = kvt[D_MODEL:, :].astype(BF16)
    kb_ref[...] = lax.dot_general(_eye(tm, BF16), kt.astype(BF16), _NT, preferred_element_type=F32).astype(BF16)
    v_ref[...] = jnp.dot(nkv, wv_ref[...], preferred_element_type=F32)
    qt = lax.dot_general(wqt_ref[...], nq, _NT, preferred_element_type=F32)
    qt = jnp.concatenate([_norm_rope_t(qt[hd * HEAD_W:(hd + 1) * HEAD_W, :], qn, cos, sin)
                          for hd in range(N_HEADS)], axis=0)
    qt_ref[0] = (qt * (MAP_W ** -0.5)).astype(BF16)


def _proj_kvq_t(h, bsz, ln_kv, ln_q, w_kvt, w_v, w_qt, kn_col, qn_col, cos_t, sin_t):
    t = h.shape[0]
    length = t // bsz
    tm = _row_tile(length, 512)
    nl = length // tm
    row = pl.BlockSpec((tm, D_MODEL), lambda b, i: (b * nl + i, 0))
    colb = pl.BlockSpec((1, D_MODEL, tm), lambda b, i: (b, 0, i))
    tab = pl.BlockSpec((ROT_DIM // 2, tm), lambda b, i: (0, i))
    t_f32 = jax.ShapeDtypeStruct((bsz, D_MODEL, length), F32)
    t_bf = jax.ShapeDtypeStruct((bsz, D_MODEL, length), BF16)
    return pl.pallas_call(
        _proj_kvq_t_kernel,
        grid=(bsz, nl),
        in_specs=[row, _const_spec((1, D_MODEL)), _const_spec((1, D_MODEL)), _const_spec(w_kvt.shape),
                  _const_spec(w_v.shape), _const_spec(w_qt.shape), _const_spec((HEAD_W, LANES)),
                  _const_spec((HEAD_W, LANES)), tab, tab],
        out_specs=[colb, row, row, colb, colb],
        out_shape=[t_f32, jax.ShapeDtypeStruct((t, D_MODEL), F32), jax.ShapeDtypeStruct((t, D_MODEL), BF16), t_bf, t_bf],
        compiler_params=_params("parallel", "parallel"),
        name="proj_kvq_t",
    )(h, ln_kv, ln_q, w_kvt, w_v, w_qt, kn_col, qn_col, cos_t, sin_t)


def _lam_from(lamv_ref, lam_init):
    lv = lamv_ref[...]
    s1 = jnp.sum(lv[0:1, :] * lv[1:2, :], axis=-1, keepdims=True)
    s2 = jnp.sum(lv[2:3, :] * lv[3:4, :], axis=-1, keepdims=True)
    return jnp.exp(s1) - jnp.exp(s2) + lam_init


def _attn_prompt_kernel(qt_ref, k_ref, vt_ref, lamv_ref, subln_ref, o_ref, m_ref, l_ref, acc_ref, *, bq, strip, lam_init):
    i = pl.program_id(2)
    qt = qt_ref[0]
    row = lax.broadcasted_iota(jnp.int32, (HEAD_W, bq), 0)
    zero = jnp.zeros_like(qt)
    q2t = jnp.concatenate([jnp.where(row < MAP_W, qt, zero), jnp.where(row >= MAP_W, qt, zero)], axis=1)
    m_ref[...] = jnp.full(m_ref.shape, -jnp.inf, F32)
    l_ref[...] = jnp.zeros_like(l_ref)
    acc_ref[...] = jnp.zeros_like(acc_ref)

    def step(j, masked):
        r0 = pl.multiple_of(j * bq, bq)
        kb = k_ref[0, pl.ds(r0, bq), :]
        vtb = vt_ref[0, :, pl.ds(r0, bq)]
        strips = [slice(t * strip, (t + 1) * strip) for t in range(2 * bq // strip)]
        s_all = [jnp.dot(kb, q2t[:, sl], preferred_element_type=F32) for sl in strips]
        for sl, s in zip(strips, s_all):
            if masked:
                ki = lax.broadcasted_iota(jnp.int32, (bq, strip), 0)
                qi = (lax.broadcasted_iota(jnp.int32, (bq, strip), 1) + sl.start) % bq
                s = jnp.where(ki <= qi, s, -jnp.inf)
            m_old = m_ref[:, sl]
            m_new = jnp.maximum(m_old, jnp.max(s, axis=0, keepdims=True))
            alpha = jnp.exp(m_old - m_new)
            p = jnp.exp(s - m_new)
            l_ref[:, sl] = alpha * l_ref[:, sl] + jnp.sum(p, axis=0, keepdims=True)
            acc_ref[:, sl] = alpha * acc_ref[:, sl] + jnp.dot(vtb, p.astype(BF16), preferred_element_type=F32)
            m_ref[:, sl] = m_new

    def body(j, carry):
        step(j, False)
        return carry

    lax.fori_loop(0, i, body, 0)
    step(i, True)

    lam = _lam_from(lamv_ref, lam_init)
    o_all = acc_ref[...] / l_ref[...]
    ot = o_all[:, :bq] - lam * o_all[:, bq:]
    gain = jnp.concatenate([subln_ref[...]] * (bq // LANES), axis=1)
    ot = ot * lax.rsqrt(jnp.mean(ot * ot, axis=0, keepdims=True) + EPS) * gain * (1.0 - lam_init)
    o = lax.dot_general(_eye(bq, BF16), ot.astype(BF16), _NT, preferred_element_type=F32)
    o_ref[0] = o.astype(o_ref.dtype)


def _attn_prompt(qt, k, vt, lamv, subln_col, lam_init):
    bsz, length, _ = k.shape
    bq = min(512, length)
    assert length % bq == 0
    return pl.pallas_call(
        functools.partial(_attn_prompt_kernel, bq=bq, strip=min(256, bq), lam_init=lam_init),
        grid=(bsz, N_HEADS, length // bq),
        in_specs=[pl.BlockSpec((1, HEAD_W, bq), lambda b, h, i: (b, h, i)),
                  pl.BlockSpec((1, length, HEAD_W), lambda b, h, i: (b, 0, h)),
                  pl.BlockSpec((1, HEAD_W, length), lambda b, h, i: (b, h, 0)),
                  _const_spec(lamv.shape), _const_spec((HEAD_W, LANES))],
        out_specs=pl.BlockSpec((1, bq, HEAD_W), lambda b, h, i: (b, i, h)),
        out_shape=jax.ShapeDtypeStruct((bsz, length, D_MODEL), BF16),
        scratch_shapes=[pltpu.VMEM((1, 2 * bq), F32), pltpu.VMEM((1, 2 * bq), F32), pltpu.VMEM((HEAD_W, 2 * bq), F32)],
        compiler_params=_params("parallel", "parallel", "arbitrary"),
        name="attn_prompt",
    )(qt, k, vt, lamv, subln_col)


def _attn_sample_kernel(pt_ref, q_ref, *refs, n_q, n_pages, lam_init):
    del pt_ref
    kc_refs = refs[:n_pages]
    vc_refs = refs[n_pages:2 * n_pages]
    kn_ref, vn_ref, lamv_ref, subln_ref, o_ref = refs[2 * n_pages:]
    grp = 2 * n_q
    q = q_ref[0].astype(F32)
    lane = lax.broadcasted_iota(jnp.int32, (n_q, HEAD_W), 1)
    r_q = lax.broadcasted_iota(jnp.int32, (grp, PAGE), 0) % n_q
    k_i = lax.broadcasted_iota(jnp.int32, (grp, PAGE), 1)
    new_mask = k_i <= r_q
    pad = jnp.zeros((PAGE - n_q, HEAD_W), BF16)
    lam = _lam_from(lamv_ref, lam_init)
    for h in range(N_HEADS):
        cols = slice(h * HEAD_W, (h + 1) * HEAD_W)
        qh = q[:, cols]
        qh2 = jnp.concatenate([jnp.where(lane < MAP_W, qh, 0.0), jnp.where(lane >= MAP_W, qh, 0.0)], axis=0).astype(BF16)
        s_pages = [jnp.dot(qh2, kc_refs[pg][0, h].astype(BF16), preferred_element_type=F32) for pg in range(n_pages)]
        k_new = jnp.concatenate([kn_ref[0, :, cols], pad], axis=0)
        s_new = jnp.where(new_mask, lax.dot_general(qh2, k_new, _NT, preferred_element_type=F32), -jnp.inf)
        s = jnp.concatenate(s_pages + [s_new], axis=1)
        m = jnp.max(s, axis=-1, keepdims=True)
        p = jnp.exp(s - m)
        l = jnp.sum(p, axis=-1, keepdims=True)
        pb = p.astype(BF16)
        v_new = jnp.concatenate([vn_ref[0, :, cols], pad], axis=0)
        acc = jnp.dot(pb[:, n_pages * PAGE:], v_new, preferred_element_type=F32)
        for pg in range(n_pages):
            v_h = vc_refs[pg][0, pl.ds(h, PAGE, stride=N_HEADS), :].astype(BF16)
            acc = acc + jnp.dot(pb[:, pg * PAGE:(pg + 1) * PAGE], v_h, preferred_element_type=F32)
        o_all = acc / l
        o = o_all[:n_q] - lam * o_all[n_q:]
        o_ref[0, :, cols] = (_rms(o, subln_ref[...]) * (1.0 - lam_init)).astype(o_ref.dtype)


def _attn_sample(q, cache_kt, cache_v2, page_table, k_new, v_new, lamv, subln, lam_init):
    bsz, n_q, _ = q.shape
    n_pages = page_table.shape[1]
    per_b = lambda b, pt: (b, 0, 0)
    k_specs = [pl.BlockSpec((1, N_HEADS, HEAD_W, PAGE), functools.partial(lambda b, pt, pg: (pt[b, pg], 0, 0, 0), pg=pg))
               for pg in range(n_pages)]
    v_specs = [pl.BlockSpec((1, PAGE * N_HEADS, HEAD_W), functools.partial(lambda b, pt, pg: (pt[b, pg], 0, 0), pg=pg))
               for pg in range(n_pages)]
    grid_spec = pltpu.PrefetchScalarGridSpec(
        num_scalar_prefetch=1,
        grid=(bsz,),
        in_specs=[pl.BlockSpec((1, n_q, D_MODEL), per_b)] + k_specs + v_specs
                 + [pl.BlockSpec((1, n_q, D_MODEL), per_b), pl.BlockSpec((1, n_q, D_MODEL), per_b),
                    pl.BlockSpec(lamv.shape, lambda b, pt: (0, 0)), pl.BlockSpec((1, HEAD_W), lambda b, pt: (0, 0))],
        out_specs=pl.BlockSpec((1, n_q, D_MODEL), per_b),
    )
    return pl.pallas_call(
        functools.partial(_attn_sample_kernel, n_q=n_q, n_pages=n_pages, lam_init=lam_init),
        grid_spec=grid_spec,
        out_shape=jax.ShapeDtypeStruct((bsz, n_q, D_MODEL), F32),
        compiler_params=_params("parallel"),
        name="attn_sample",
    )(page_table, q, *([cache_kt] * n_pages), *([cache_v2] * n_pages), k_new, v_new, lamv, subln)


def _rope_angles(pos):
    inv = ROPE_THETA ** (-jnp.arange(0, ROT_DIM, 2, dtype=F32) / ROT_DIM)
    ang = pos.astype(F32)[:, None] * inv[None, :]
    return jnp.cos(ang), jnp.sin(ang)


def _rope_tables(pos):
    half = ROT_DIM // 2
    cos, sin = _rope_angles(pos)
    n = pos.shape[0]
    ones = jnp.ones((n, MAP_W - ROT_DIM), F32)
    zeros = jnp.zeros((n, MAP_W - ROT_DIM), F32)
    z8 = jnp.zeros((n, half), F32)
    cos_g = jnp.concatenate([cos, cos, ones], axis=1)
    sa_g = jnp.concatenate([-sin, z8, zeros], axis=1)
    sb_g = jnp.concatenate([z8, sin, zeros], axis=1)
    rep = LANES // MAP_W
    return jnp.tile(cos_g, (1, rep)), jnp.tile(sa_g, (1, rep)), jnp.tile(sb_g, (1, rep))


def _pad_lanes(v):
    return jnp.zeros((1, LANES), F32).at[0, :v.shape[0]].set(v.astype(F32))


def _gain_col(g):
    return jnp.broadcast_to(jnp.tile(g.astype(F32), HEAD_W // MAP_W)[:, None], (HEAD_W, LANES))


def kernel(x_prompt, x_sample, cache_k, cache_v, page_table, state_delta, state_conv, ln_mix, ln_mlp, w_up, w_down,
           w_in_a, conv_a, a_log, dt_bias, o_norm_a, w_out_a, ln_kv, w_kv, k_norm, w_q_b, q_norm_b,
           lam_q1, lam_k1, lam_q2, lam_k2, subln_b, w_out_b):
    bsz, seq, _ = x_prompt.shape
    dbsz, dseq, _ = x_sample.shape
    n_pages = page_table.shape[1]
    past_len = n_pages * PAGE
    tp, ts = bsz * seq, dbsz * dseq
    row2 = lambda v: v.reshape(1, -1).astype(F32)

    w_in = w_in_a[0]
    w_main = w_in[:, :CONV_W + D_MODEL].astype(BF16)
    w_ba = jnp.zeros((D_MODEL, 2 * LANES), F32)
    w_ba = w_ba.at[:, :N_HEADS].set(w_in[:, CONV_W + D_MODEL:CONV_W + D_MODEL + N_HEADS])
    w_ba = w_ba.at[:, LANES:LANES + N_HEADS].set(w_in[:, CONV_W + D_MODEL + N_HEADS:]).astype(BF16)
    ln0 = row2(ln_mix[0])
    alog, dtb, onorm = _pad_lanes(a_log[0]), _pad_lanes(dt_bias[0]), row2(o_norm_a[0])

    xp2, xs2 = x_prompt.reshape(tp, D_MODEL), x_sample.reshape(ts, D_MODEL)
    qkv_p, z_p, ba_p = _proj_in(xp2, ln0, w_main, w_ba)
    qkv_s, z_s, ba_s = _proj_in(xs2, ln0, w_main, w_ba)

    nchunk_p = min(4, seq // DELTA_CHUNK)
    o_p, sd_p, ct_p = _delta(qkv_p.reshape(bsz, seq, CONV_W), z_p.reshape(bsz, seq, D_MODEL),
                             ba_p.reshape(bsz, seq, 2 * LANES), conv_a[0], alog, dtb, onorm,
                             chunk=DELTA_CHUNK, n_chunks=nchunk_p, group=min(2, nchunk_p))
    chunk_s = math.gcd(dseq, DELTA_CHUNK)
    cbuf = jnp.concatenate([jnp.zeros((dbsz, SUBLANES - (CONV_TAPS - 1), CONV_W), F32), state_conv[0]], axis=1)
    o_s, sd_s, ct_s = _delta(qkv_s.reshape(dbsz, dseq, CONV_W), z_s.reshape(dbsz, dseq, D_MODEL),
                             ba_s.reshape(dbsz, dseq, 2 * LANES), conv_a[0], alog, dtb, onorm,
                             chunk=chunk_s, n_chunks=dseq // chunk_s, bb=math.gcd(dbsz, 4),
                             s0=state_delta[0], conv_buf=cbuf, out_dtype=F32)

    wo_a, wup0, wdn0 = w_out_a[0].astype(BF16), w_up[0].astype(BF16), w_down[0].astype(BF16)
    h_p = _post(xp2, o_p.reshape(tp, D_MODEL), wo_a, row2(ln_mlp[0]), wup0, wdn0)
    h_s = _post(xs2, o_s.reshape(ts, D_MODEL), wo_a, row2(ln_mlp[0]), wup0, wdn0)

    lam_init = 0.8 - 0.6 * math.exp(-0.3 * 1)
    wkv, wq = w_kv.astype(BF16), w_q_b[0].astype(BF16)
    ln1 = row2(ln_mix[1])
    cos_p, sin_p = _rope_angles(jnp.arange(seq, dtype=jnp.int32))
    kt_p, v_p, kb_p, vt_p, qt_p = _proj_kvq_t(h_p, bsz, row2(ln_kv), ln1, wkv.T, wkv[:, D_MODEL:], wq.T,
                                              _gain_col(k_norm), _gain_col(q_norm_b[0]), cos_p.T, sin_p.T)
    kn = jnp.tile(k_norm.astype(F32), LANES // MAP_W).reshape(1, LANES)
    qn = jnp.tile(q_norm_b[0].astype(F32), LANES // MAP_W).reshape(1, LANES)
    gi = jnp.arange(LANES) // MAP_W
    gmat = (gi[:, None] == gi[None, :]).astype(F32) / MAP_W
    pos_s = past_len + jnp.arange(dseq, dtype=jnp.int32)
    tile_s = min(512, ts) // dseq
    cos_s, sa_s, sb_s = (jnp.tile(t, (tile_s, 1)) for t in _rope_tables(pos_s))
    k_s, v_s, kb_s, vb_s, qb_s = _proj_kvq(h_s, row2(ln_kv), ln1, wkv, wq, kn, qn, gmat, cos_s, sa_s, sb_s)

    lamv = jnp.zeros((SUBLANES, LANES), F32)
    for r, vec in enumerate((lam_q1[0], lam_k1[0], lam_q2[0], lam_k2[0])):
        lamv = lamv.at[r, :MAP_W].set(vec.astype(F32))
    subln = row2(subln_b[0])
    subln_col = jnp.broadcast_to(subln_b[0].astype(F32)[:, None], (HEAD_W, LANES))
    a_p = _attn_prompt(qt_p, kb_p.reshape(bsz, seq, D_MODEL), vt_p, lamv, subln_col, lam_init)
    n_pool = cache_k.shape[0]
    cache_kt = jnp.transpose(cache_k, (0, 2, 3, 4, 1)).reshape(n_pool, N_HEADS, HEAD_W, PAGE)
    cache_v2 = cache_v.reshape(n_pool, PAGE * N_HEADS, HEAD_W)
    a_s = _attn_sample(qb_s.reshape(dbsz, dseq, D_MODEL), cache_kt, cache_v2, page_table,
                       kb_s.reshape(dbsz, dseq, D_MODEL), vb_s.reshape(dbsz, dseq, D_MODEL), lamv, subln, lam_init)

    wo_b, wup1, wdn1 = w_out_b[0].astype(BF16), w_up[1].astype(BF16), w_down[1].astype(BF16)
    y_p = _post(h_p, a_p.reshape(tp, D_MODEL), wo_b, row2(ln_mlp[1]), wup1, wdn1)
    y_s = _post(h_s, a_s.reshape(ts, D_MODEL), wo_b, row2(ln_mlp[1]), wup1, wdn1)

    tail = slice(SUBLANES - (CONV_TAPS - 1), SUBLANES)
    k_prompt = jnp.transpose(kt_p.reshape(bsz, N_HEADS, 2, MAP_W, seq), (0, 4, 1, 2, 3))
    return (y_p.reshape(bsz, seq, D_MODEL), y_s.reshape(dbsz, dseq, D_MODEL),
            k_prompt, v_p.reshape(bsz, seq, N_HEADS, HEAD_W),
            k_s.reshape(dbsz, dseq, N_HEADS, 2, MAP_W), v_s.reshape(dbsz, dseq, N_HEADS, HEAD_W),
            sd_p[None], sd_s[None], ct_p[:, tail][None], ct_s[:, tail][None])
```

```python
import functools
import math

import jax
import jax.numpy as jnp
from jax import lax
from jax.experimental import pallas as pl
from jax.experimental.pallas import tpu as pltpu

F32 = jnp.float32
BF16 = jnp.bfloat16
EPS = 1e-6

D_MODEL = 1024
D_FF = 4 * D_MODEL
N_HEADS = 8
HEAD_W = 128
MAP_W = 64
CONV_TAPS = 4
CONV_W = 3 * D_MODEL
DELTA_CHUNK = 64
ROT_DIM = 16
ROPE_THETA = 500000.0
PAGE = 128
LANES = 128
SUBLANES = 8
ONES_ROWS = 16
LOG2E = math.log2(math.e)
VMEM_LIMIT = 56 * 1024 * 1024

_NT = (((1,), (1,)), ((), ()))
_HI = lax.Precision.HIGHEST


def _dot(a, b):
    return jnp.dot(a.astype(BF16), b.astype(BF16), preferred_element_type=F32)


def _dot_nt(a, b):
    return lax.dot_general(a.astype(BF16), b.astype(BF16), _NT, preferred_element_type=F32)


def _rms(x, g):
    return x * lax.rsqrt(jnp.mean(x * x, axis=-1, keepdims=True) + EPS) * g


def _sigmoid(x):
    return 1.0 / (1.0 + jnp.exp(-x))


def _softplus(x):
    return jnp.maximum(x, 0.0) + jnp.log1p(jnp.exp(-jnp.abs(x)))


def _eye(n, dtype):
    return (lax.broadcasted_iota(jnp.int32, (n, n), 0) == lax.broadcasted_iota(jnp.int32, (n, n), 1)).astype(dtype)


def _const_spec(shape):
    nd = len(shape)
    return pl.BlockSpec(shape, lambda *_: (0,) * nd, pipeline_mode=pl.Buffered(1))


def _params(*sem):
    return pltpu.CompilerParams(dimension_semantics=sem, vmem_limit_bytes=VMEM_LIMIT)


def _row_tile(t, want):
    tm = min(t, want)
    assert t % tm == 0
    return tm


def _proj_in_kernel(x_ref, ln_ref, w_ref, wba_ref, qkv_ref, z_ref, ba_ref):
    h = _rms(x_ref[...], ln_ref[...]).astype(BF16)
    qkv_ref[...] = jnp.dot(h, w_ref[:, :CONV_W], preferred_element_type=F32)
    z_ref[...] = jnp.dot(h, w_ref[:, CONV_W:], preferred_element_type=F32)
    ba_ref[...] = jnp.dot(h, wba_ref[...], preferred_element_type=F32)


def _proj_in(x, ln, w_main, w_ba):
    t = x.shape[0]
    tm = _row_tile(t, 512)
    row = lambda w: pl.BlockSpec((tm, w), lambda i: (i, 0))
    return pl.pallas_call(
        _proj_in_kernel,
        grid=(t // tm,),
        in_specs=[row(D_MODEL), _const_spec((1, D_MODEL)), _const_spec(w_main.shape), _const_spec(w_ba.shape)],
        out_specs=[row(CONV_W), row(D_MODEL), row(2 * LANES)],
        out_shape=[jax.ShapeDtypeStruct((t, CONV_W), F32), jax.ShapeDtypeStruct((t, D_MODEL), F32),
                   jax.ShapeDtypeStruct((t, 2 * LANES), F32)],
        compiler_params=_params("parallel"),
        name="proj_in",
    )(x, ln, w_main, w_ba)


def _delta_kernel(*refs, chunk, n_chunks, bb, group, has_init):
    if has_init:
        (qkv_ref, z_ref, ba_ref, s0_ref, cb_ref, cw_ref, alog_ref, dtb_ref, onorm_ref,
         o_ref, sfin_ref, ctail_ref,
         xp_ref, q_s, k_s, v_s, g_s, beta_s, s_ref, wq_s, ub_s, at_s, kd_s, ee_s) = refs
    else:
        (qkv_ref, z_ref, ba_ref, cw_ref, alog_ref, dtb_ref, onorm_ref,
         o_ref, sfin_ref, ctail_ref,
         xp_ref, q_s, k_s, v_s, g_s, beta_s, s_ref, wq_s, ub_s, at_s, kd_s, ee_s) = refs
    c = chunk
    blk = c * n_chunks
    i = pl.program_id(1)
    last = pl.num_programs(1) - 1
    halo = SUBLANES

    @pl.when(i == 0)
    def _():
        if has_init:
            s_ref[...] = s0_ref[...]
            xp_ref[:, 0:halo, :] = cb_ref[...]
        else:
            s_ref[...] = jnp.zeros_like(s_ref)
            xp_ref[:, 0:halo, :] = jnp.zeros((bb, halo, CONV_W), F32)

    for bi in range(bb):
        xp_ref[bi, halo:halo + blk, :] = qkv_ref[bi]
        for j in range(CONV_W // LANES):
            cols = slice(j * LANES, (j + 1) * LANES)
            acc = None
            for w in range(CONV_TAPS):
                r0 = halo - (CONV_TAPS - 1) + w
                term = xp_ref[bi, r0:r0 + blk, cols] * cw_ref[w:w + 1, cols]
                acc = term if acc is None else acc + term
            y = acc * _sigmoid(acc)
            if j < 2 * N_HEADS:
                y = y * lax.rsqrt(jnp.sum(y * y, axis=-1, keepdims=True) + EPS)
            if j < N_HEADS:
                q_s[bi, :, cols] = y * (HEAD_W ** -0.5)
            elif j < 2 * N_HEADS:
                k_s[bi, :, (j - N_HEADS) * LANES:(j - N_HEADS + 1) * LANES] = y
            else:
                v_s[bi, :, (j - 2 * N_HEADS) * LANES:(j - 2 * N_HEADS + 1) * LANES] = y
        tail = xp_ref[bi, blk:blk + halo, :]
        xp_ref[bi, 0:halo, :] = tail

        @pl.when(i == last)
        def _():
            ctail_ref[bi] = tail

        ba = ba_ref[bi]
        beta_s[bi] = _sigmoid(ba[:, :LANES])
        g_s[bi] = -jnp.exp(alog_ref[...]) * _softplus(ba[:, LANES:] + dtb_ref[...])

    row = lax.broadcasted_iota(jnp.int32, (c, c), 0)
    col = lax.broadcasted_iota(jnp.int32, (c, c), 1)
    incl = row >= col
    strict = row > col
    ltri = incl.astype(F32)
    utri = (row <= col).astype(F32)
    eye_c = (row == col).astype(F32)
    eye_f = _eye(LANES, F32)
    eye_b = eye_f.astype(BF16)
    onorm = onorm_ref[...]
    n_sq = int(math.log2(c)) - 1
    heads = range(N_HEADS)
    hcols = [slice(h * LANES, (h + 1) * LANES) for h in heads]

    def phase_a(groups):
        inst = [(gi, h) for gi in range(len(groups)) for h in heads]
        rows, cg, cg_t, beta_c, ecg, ekd, bec = [], [], [], [], [], [], []
        for bi, ci in groups:
            r0 = ci * c if isinstance(ci, int) else pl.multiple_of(ci * c, c)
            rs = pl.ds(r0, c)
            g_c = g_s[bi, rs, :]
            b_c = beta_s[bi, rs, :]
            cg_c = jnp.dot(ltri, g_c, preferred_element_type=F32, precision=_HI)
            g_t = lax.dot_general(eye_f, g_c, _NT, preferred_element_type=F32, precision=_HI)
            cg_t.append(jnp.dot(g_t, utri, preferred_element_type=F32, precision=_HI))
            cg_last = cg_c[c - 1:c, :]
            e_c = jnp.exp(cg_c)
            ee_s[bi, ci] = jnp.exp(cg_last)
            rows.append(rs)
            cg.append(cg_c)
            beta_c.append(b_c)
            ecg.append(e_c)
            ekd.append(jnp.exp(cg_last - cg_c))
            bec.append(b_c * e_c)
        q = [q_s[groups[gi][0], rows[gi], hcols[h]] for gi, h in inst]
        k = [k_s[groups[gi][0], rows[gi], hcols[h]] for gi, h in inst]
        v = [v_s[groups[gi][0], rows[gi], hcols[h]] for gi, h in inst]
        kb = [x.astype(BF16) for x in k]
        kk = [_dot_nt(x, x) for x in kb]
        qk = [_dot_nt(a, b) for a, b in zip(q, kb)]
        beta_h = [beta_c[gi][:, h:h + 1] for gi, h in inst]
        diff = [cg[gi][:, h:h + 1] - cg_t[gi][h:h + 1, :] for gi, h in inst]
        decay = [jnp.where(incl, jnp.exp(jnp.where(incl, d, 0.0)), 0.0) for d in diff]
        nmat = [-(jnp.where(strict, dc * x, 0.0) * b) for dc, x, b in zip(decay, kk, beta_h)]
        pmat = [eye_c + n for n in nmat]
        qmat = [_dot(n, n) for n in nmat]
        for it in range(n_sq):
            if it < n_sq - 1:
                pq = [_dot(jnp.concatenate([p, n], axis=0), n) for p, n in zip(pmat, qmat)]
                pmat = [p + x[:c] for p, x in zip(pmat, pq)]
                qmat = [x[c:] for x in pq]
            else:
                pmat = [p + _dot(p, n) for p, n in zip(pmat, qmat)]
        rhs = [jnp.concatenate([kx * bec[gi][:, h:h + 1], vx * b], axis=1)
               for (gi, h), kx, vx, b in zip(inst, k, v, beta_h)]
        sol = [_dot(p, r) for p, r in zip(pmat, rhs)]
        k_dec = [kx * ekd[gi][:, h:h + 1] for (gi, h), kx in zip(inst, k)]
        kd_t = [lax.dot_general(eye_b, x.astype(BF16), _NT, preferred_element_type=F32) for x in k_dec]
        for n, (gi, h) in enumerate(inst):
            bi, ci = groups[gi]
            q_dec = q[n] * ecg[gi][:, h:h + 1]
            wq_s[bi, ci, h] = jnp.concatenate([sol[n][:, :LANES], q_dec], axis=0).astype(BF16)
            ub_s[bi, ci, h] = sol[n][:, LANES:]
            at_s[bi, ci, h] = decay[n] * qk[n]
            kd_s[bi, ci, h] = kd_t[n]

    def phase_b(ci):
        inst = [(bi, h) for bi in range(bb) for h in heads]
        r0 = ci * c if isinstance(ci, int) else pl.multiple_of(ci * c, c)
        rs = pl.ds(r0, c)
        s_old = [s_ref[bi, h] for bi, h in inst]
        r = [jnp.dot(wq_s[bi, ci, h], s.astype(BF16), preferred_element_type=F32) for (bi, h), s in zip(inst, s_old)]
        u = [ub_s[bi, ci, h] - x[:c] for (bi, h), x in zip(inst, r)]
        ub16 = [x.astype(BF16) for x in u]
        o = [x[c:] + _dot(at_s[bi, ci, h], y) for (bi, h), x, y in zip(inst, r, ub16)]
        for (bi, h), s, y in zip(inst, s_old, ub16):
            s_ref[bi, h] = s * ee_s[bi, ci][:, h:h + 1] + _dot(kd_s[bi, ci, h], y)
        for (bi, h), x in zip(inst, o):
            zt = z_ref[bi, rs, hcols[h]]
            o_ref[bi, rs, hcols[h]] = (_rms(x, onorm) * (zt * _sigmoid(zt))).astype(o_ref.dtype)

    if n_chunks == 1:
        phase_a([(bi, 0) for bi in range(bb)])
        phase_b(0)
    else:
        def a_body(gi, carry):
            phase_a([(bi, gi * group + t) for bi in range(bb) for t in range(group)])
            return carry

        def b_body(ci, carry):
            phase_b(ci)
            return carry

        lax.fori_loop(0, n_chunks // group, a_body, 0)
        lax.fori_loop(0, n_chunks, b_body, 0)

    @pl.when(i == last)
    def _():
        sfin_ref[...] = s_ref[...]


def _delta(qkv, z, ba, conv_w, a_log, dt_bias, o_norm, *, chunk, n_chunks, bb=1, group=1,
           s0=None, conv_buf=None, out_dtype=BF16):
    bsz, length, _ = qkv.shape
    blk = chunk * n_chunks
    assert length % blk == 0 and blk >= SUBLANES and bsz % bb == 0 and n_chunks % group == 0
    has_init = s0 is not None
    seq = lambda w: pl.BlockSpec((bb, blk, w), lambda b, i: (b, i, 0))
    state = pl.BlockSpec((bb, N_HEADS, HEAD_W, HEAD_W), lambda b, i: (b, 0, 0, 0))
    ctail = pl.BlockSpec((bb, SUBLANES, CONV_W), lambda b, i: (b, 0, 0))
    in_specs = [seq(CONV_W), seq(D_MODEL), seq(2 * LANES)]
    args = [qkv, z, ba]
    if has_init:
        in_specs += [state, ctail]
        args += [s0, conv_buf]
    in_specs += [_const_spec(conv_w.shape), _const_spec((1, LANES)), _const_spec((1, LANES)), _const_spec((1, HEAD_W))]
    args += [conv_w, a_log, dt_bias, o_norm]
    per = (bb, n_chunks, N_HEADS)
    return pl.pallas_call(
        functools.partial(_delta_kernel, chunk=chunk, n_chunks=n_chunks, bb=bb, group=group, has_init=has_init),
        grid=(bsz // bb, length // blk),
        in_specs=in_specs,
        out_specs=[seq(D_MODEL), state, ctail],
        out_shape=[jax.ShapeDtypeStruct((bsz, length, D_MODEL), out_dtype),
                   jax.ShapeDtypeStruct((bsz, N_HEADS, HEAD_W, HEAD_W), F32),
                   jax.ShapeDtypeStruct((bsz, SUBLANES, CONV_W), F32)],
        scratch_shapes=[pltpu.VMEM((bb, blk + SUBLANES, CONV_W), F32),
                        pltpu.VMEM((bb, blk, D_MODEL), F32), pltpu.VMEM((bb, blk, D_MODEL), F32),
                        pltpu.VMEM((bb, blk, D_MODEL), F32),
                        pltpu.VMEM((bb, blk, LANES), F32), pltpu.VMEM((bb, blk, LANES), F32),
                        pltpu.VMEM((bb, N_HEADS, HEAD_W, HEAD_W), F32),
                        pltpu.VMEM(per + (2 * chunk, HEAD_W), BF16), pltpu.VMEM(per + (chunk, HEAD_W), F32),
                        pltpu.VMEM(per + (chunk, chunk), F32), pltpu.VMEM(per + (HEAD_W, chunk), F32),
                        pltpu.VMEM((bb, n_chunks, 1, LANES), F32)],
        compiler_params=_params("parallel", "arbitrary"),
        name="delta_rule",
    )(*args)


def _post_kernel(x_ref, o_ref, wo_ref, ln_ref, wup_ref, wdn_ref, y_ref, *, ff_chunk):
    h1 = x_ref[...] + jnp.dot(o_ref[...].astype(BF16), wo_ref[...], preferred_element_type=F32)
    n = _rms(h1, ln_ref[...]).astype(BF16)
    acc = h1
    for c0 in range(0, D_FF, ff_chunk):
        u = jnp.dot(n, wup_ref[:, c0:c0 + ff_chunk], preferred_element_type=F32)
        a = jnp.square(jnp.maximum(u, 0.0)).astype(BF16)
        acc = acc + jnp.dot(a, wdn_ref[c0:c0 + ff_chunk, :], preferred_element_type=F32)
    y_ref[...] = acc


def _post(x, o, w_out, ln, w_up, w_down):
    t = x.shape[0]
    tm = _row_tile(t, 512)
    row = pl.BlockSpec((tm, D_MODEL), lambda i: (i, 0))
    return pl.pallas_call(
        functools.partial(_post_kernel, ff_chunk=1024),
        grid=(t // tm,),
        in_specs=[row, row, _const_spec(w_out.shape), _const_spec((1, D_MODEL)),
                  _const_spec(w_up.shape), _const_spec(w_down.shape)],
        out_specs=row,
        out_shape=jax.ShapeDtypeStruct((t, D_MODEL), F32),
        compiler_params=_params("parallel"),
        name="post_mlp",
    )(x, o, w_out, ln, w_up, w_down)


def _head_norm_rope(x, gain, gmat, cos, sin_a, sin_b):
    outs = []
    for j in range(D_MODEL // LANES):
        xt = x[:, j * LANES:(j + 1) * LANES]
        ms = jnp.dot(xt * xt, gmat, preferred_element_type=F32, precision=_HI)
        xn = xt * lax.rsqrt(ms + EPS) * gain
        outs.append(xn * cos + pltpu.roll(xn, LANES - ROT_DIM // 2, axis=1) * sin_a
                    + pltpu.roll(xn, ROT_DIM // 2, axis=1) * sin_b)
    return jnp.concatenate(outs, axis=1)


def _proj_kvq_kernel(h_ref, lnkv_ref, lnq_ref, wkv_ref, wq_ref, kn_ref, qn_ref, gmat_ref, cos_ref, sa_ref, sb_ref,
                     k_ref, v_ref, kb_ref, vb_ref, qb_ref):
    h = h_ref[...]
    cos, sin_a, sin_b, gmat = cos_ref[...], sa_ref[...], sb_ref[...], gmat_ref[...]
    nkv = _rms(h, lnkv_ref[...]).astype(BF16)
    k = jnp.dot(nkv, wkv_ref[:, :D_MODEL], preferred_element_type=F32)
    v = jnp.dot(nkv, wkv_ref[:, D_MODEL:], preferred_element_type=F32)
    k = _head_norm_rope(k, kn_ref[...], gmat, cos, sin_a, sin_b)
    k_ref[...] = k
    v_ref[...] = v
    kb_ref[...] = k.astype(BF16)
    vb_ref[...] = v.astype(BF16)
    nq = _rms(h, lnq_ref[...]).astype(BF16)
    q = jnp.dot(nq, wq_ref[...], preferred_element_type=F32)
    q = _head_norm_rope(q, qn_ref[...], gmat, cos, sin_a, sin_b)
    qb_ref[...] = (q * (MAP_W ** -0.5)).astype(BF16)


def _proj_kvq(h, ln_kv, ln_q, w_kv, w_q, k_norm, q_norm, gmat, cos, sin_a, sin_b):
    t = h.shape[0]
    tm = _row_tile(t, min(512, cos.shape[0]))
    n_tab = cos.shape[0] // tm
    row = pl.BlockSpec((tm, D_MODEL), lambda i: (i, 0))
    tab = pl.BlockSpec((tm, LANES), lambda i: (i % n_tab, 0))
    f32_out = jax.ShapeDtypeStruct((t, D_MODEL), F32)
    bf_out = jax.ShapeDtypeStruct((t, D_MODEL), BF16)
    return pl.pallas_call(
        _proj_kvq_kernel,
        grid=(t // tm,),
        in_specs=[row, _const_spec((1, D_MODEL)), _const_spec((1, D_MODEL)), _const_spec(w_kv.shape), _const_spec(w_q.shape),
                  _const_spec((1, LANES)), _const_spec((1, LANES)), _const_spec((LANES, LANES)), tab, tab, tab],
        out_specs=[row] * 5,
        out_shape=[f32_out, f32_out, bf_out, bf_out, bf_out],
        compiler_params=_params("parallel"),
        name="proj_kvq",
    )(h, ln_kv, ln_q, w_kv, w_q, k_norm, q_norm, gmat, cos, sin_a, sin_b)


def _norm_rope_t(x, gain, cos, sin):
    half = ROT_DIM // 2
    outs = []
    for g in range(HEAD_W // MAP_W):
        xg = x[g * MAP_W:(g + 1) * MAP_W, :]
        ms = jnp.mean(xg * xg, axis=0, keepdims=True)
        xn = xg * lax.rsqrt(ms + EPS) * gain[g * MAP_W:(g + 1) * MAP_W, :]
        x1, x2 = xn[0:half], xn[half:ROT_DIM]
        outs += [x1 * cos - x2 * sin, x2 * cos + x1 * sin, xn[ROT_DIM:]]
    return jnp.concatenate(outs, axis=0)


def _proj_kvq_t_kernel(h_ref, lnkv_ref, lnq_ref, wkvt_ref, wv_ref, wqt_ref, kn_ref, qn_ref, cos_ref, sin_ref,
                       kt_ref, v_ref, kb_ref, vt_ref, qt_ref):
    h = h_ref[...]
    tm = h.shape[0]
    hn = h * lax.rsqrt(jnp.mean(h * h, axis=-1, keepdims=True) + EPS)
    nkv = (hn * lnkv_ref[...]).astype(BF16)
    nq = (hn * lnq_ref[...]).astype(BF16)
    cos, sin = cos_ref[...], sin_ref[...]
    rep = tm // LANES
    kn = jnp.concatenate([kn_ref[...]] * rep, axis=1)
    qn = jnp.concatenate([qn_ref[...]] * rep, axis=1)
    kvt = lax.dot_general(wkvt_ref[...], nkv, _NT, preferred_element_type=F32)
    kt = jnp.concatenate([_norm_rope_t(kvt[hd * HEAD_W:(hd + 1) * HEAD_W, :], kn, cos, sin)
                          for hd in range(N_HEADS)], axis=0)
    kt_ref[0] = kt
    vt_ref[0] = kvt[D_MODEL:, :].astype(BF16)
    kb_ref[...] = lax.dot_general(_eye(tm, BF16), kt.astype(BF16), _NT, preferred_element_type=F32).astype(BF16)
    v_ref[...] = jnp.dot(nkv, wv_ref[...], preferred_element_type=F32)
    qt = lax.dot_general(wqt_ref[...], nq, _NT, preferred_element_type=F32)
    qt = jnp.concatenate([_norm_rope_t(qt[hd * HEAD_W:(hd + 1) * HEAD_W, :], qn, cos, sin)
                          for hd in range(N_HEADS)], axis=0)
    qt_ref[0] = (qt * (MAP_W ** -0.5 * LOG2E)).astype(BF16)


def _proj_kvq_t(h, bsz, ln_kv, ln_q, w_kvt, w_v, w_qt, kn_col, qn_col, cos_t, sin_t):
    t = h.shape[0]
    length = t // bsz
    tm = _row_tile(length, 512)
    nl = length // tm
    row = pl.BlockSpec((tm, D_MODEL), lambda b, i: (b * nl + i, 0))
    colb = pl.BlockSpec((1, D_MODEL, tm), lambda b, i: (b, 0, i))
    tab = pl.BlockSpec((ROT_DIM // 2, tm), lambda b, i: (0, i))
    t_f32 = jax.ShapeDtypeStruct((bsz, D_MODEL, length), F32)
    t_bf = jax.ShapeDtypeStruct((bsz, D_MODEL, length), BF16)
    return pl.pallas_call(
        _proj_kvq_t_kernel,
        grid=(bsz, nl),
        in_specs=[row, _const_spec((1, D_MODEL)), _const_spec((1, D_MODEL)), _const_spec(w_kvt.shape),
                  _const_spec(w_v.shape), _const_spec(w_qt.shape), _const_spec((HEAD_W, LANES)),
                  _const_spec((HEAD_W, LANES)), tab, tab],
        out_specs=[colb, row, row, colb, colb],
        out_shape=[t_f32, jax.ShapeDtypeStruct((t, D_MODEL), F32), jax.ShapeDtypeStruct((t, D_MODEL), BF16), t_bf, t_bf],
        compiler_params=_params("parallel", "parallel"),
        name="proj_kvq_t",
    )(h, ln_kv, ln_q, w_kvt, w_v, w_qt, kn_col, qn_col, cos_t, sin_t)


def _lam_from(lamv_ref, lam_init):
    lv = lamv_ref[...]
    s1 = jnp.sum(lv[0:1, :] * lv[1:2, :], axis=-1, keepdims=True)
    s2 = jnp.sum(lv[2:3, :] * lv[3:4, :], axis=-1, keepdims=True)
    return jnp.exp(s1) - jnp.exp(s2) + lam_init


def _attn_prompt_kernel(qt_ref, k_ref, vt_ref, lamv_ref, subln_ref, o_ref, m_ref, acc_ref,
                        s_ref, s1_ref, p_ref, p1_ref, *, bq, strip, lam_init):
    i = pl.program_id(2)
    qt = qt_ref[0]
    row = lax.broadcasted_iota(jnp.int32, (HEAD_W, bq), 0)
    zero = jnp.zeros_like(qt)
    q2t = jnp.concatenate([jnp.where(row < MAP_W, qt, zero), jnp.where(row >= MAP_W, qt, zero)], axis=1)
    strips = [slice(t * strip, (t + 1) * strip) for t in range(2 * bq // strip)]
    m_ref[...] = jnp.full(m_ref.shape, -jnp.inf, F32)
    acc_ref[...] = jnp.zeros_like(acc_ref)
    s_refs, p_refs = (s_ref, s1_ref), (p_ref, p1_ref)
    p1_ref[...] = jnp.zeros(p1_ref.shape, BF16)
    ones = jnp.ones((ONES_ROWS, bq), BF16)

    def keys(j):
        return k_ref[0, pl.ds(pl.multiple_of(j * bq, bq), bq), :]

    def values_t(j):
        vt = vt_ref[0, :, pl.ds(pl.multiple_of(j * bq, bq), bq)]
        return jnp.concatenate([vt, ones], axis=0)

    def scores(j, slot):
        kb = keys(j)
        for sl in strips:
            s_refs[slot][:, sl] = jnp.dot(kb, q2t[:, sl], preferred_element_type=F32)

    def softmax_update(j, cur, masked):
        vt_prev = values_t(jnp.maximum(j - 1, 0))
        for sl in strips:
            pv = jnp.dot(vt_prev, p_refs[1 - cur][:, sl], preferred_element_type=F32)
            s = s_refs[cur][:, sl]
            if masked:
                ki = lax.broadcasted_iota(jnp.int32, (bq, strip), 0)
                qi = (lax.broadcasted_iota(jnp.int32, (bq, strip), 1) + sl.start) % bq
                s = jnp.where(ki <= qi, s, -jnp.inf)
            m_old = m_ref[:, sl]
            m_new = jnp.maximum(m_old, jnp.max(s, axis=0, keepdims=True))
            alpha = jnp.exp2(m_old - m_new)
            pb = jnp.exp2((s - m_new).astype(BF16))
            acc = alpha * (acc_ref[:, sl] + pv)
            if masked:
                acc = acc + jnp.dot(values_t(j), pb, preferred_element_type=F32)
            else:
                p_refs[cur][:, sl] = pb
            acc_ref[:, sl] = acc
            m_ref[:, sl] = m_new

    def sub_step(j, cur):
        scores(j + 1, 1 - cur)
        softmax_update(j, cur, False)

    scores(0, 0)

    def body(t, carry):
        sub_step(2 * t, 0)
        sub_step(2 * t + 1, 1)
        return carry

    lax.fori_loop(0, i // 2, body, 0)
    odd = i % 2 == 1

    @pl.when(odd)
    def _():
        sub_step(i - 1, 0)
        softmax_update(i, 1, True)

    @pl.when(jnp.logical_not(odd))
    def _():
        softmax_update(i, 0, True)

    lam = _lam_from(lamv_ref, lam_init)
    o_all = acc_ref[0:HEAD_W, :] / acc_ref[HEAD_W:HEAD_W + 1, :]
    ot = o_all[:, :bq] - lam * o_all[:, bq:]
    gain = jnp.concatenate([subln_ref[...]] * (bq // LANES), axis=1)
    ot = ot * lax.rsqrt(jnp.mean(ot * ot, axis=0, keepdims=True) + EPS) * gain * (1.0 - lam_init)
    o = lax.dot_general(_eye(bq, BF16), ot.astype(BF16), _NT, preferred_element_type=F32)
    o_ref[0] = o.astype(o_ref.dtype)


def _attn_prompt(qt, k, vt, lamv, subln_col, lam_init):
    bsz, length, _ = k.shape
    bq = min(512, length)
    assert length % bq == 0
    return pl.pallas_call(
        functools.partial(_attn_prompt_kernel, bq=bq, strip=min(256, bq), lam_init=lam_init),
        grid=(bsz, N_HEADS, length // bq),
        in_specs=[pl.BlockSpec((1, HEAD_W, bq), lambda b, h, i: (b, h, i)),
                  pl.BlockSpec((1, length, HEAD_W), lambda b, h, i: (b, 0, h)),
                  pl.BlockSpec((1, HEAD_W, length), lambda b, h, i: (b, h, 0)),
                  _const_spec(lamv.shape), _const_spec((HEAD_W, LANES))],
        out_specs=pl.BlockSpec((1, bq, HEAD_W), lambda b, h, i: (b, i, h)),
        out_shape=jax.ShapeDtypeStruct((bsz, length, D_MODEL), BF16),
        scratch_shapes=[pltpu.VMEM((1, 2 * bq), F32), pltpu.VMEM((HEAD_W + ONES_ROWS, 2 * bq), F32),
                        pltpu.VMEM((bq, 2 * bq), F32), pltpu.VMEM((bq, 2 * bq), F32),
                        pltpu.VMEM((bq, 2 * bq), BF16), pltpu.VMEM((bq, 2 * bq), BF16)],
        compiler_params=_params("parallel", "parallel", "arbitrary"),
        name="attn_prompt",
    )(qt, k, vt, lamv, subln_col)


def _attn_sample_kernel(pt_ref, q_ref, *refs, n_q, n_pages, lam_init):
    del pt_ref
    kc_refs = refs[:n_pages]
    vc_refs = refs[n_pages:2 * n_pages]
    kn_ref, vn_ref, lamv_ref, subln_ref, o_ref = refs[2 * n_pages:]
    grp = 2 * n_q
    q = q_ref[0].astype(F32)
    lane = lax.broadcasted_iota(jnp.int32, (n_q, HEAD_W), 1)
    r_q = lax.broadcasted_iota(jnp.int32, (grp, PAGE), 0) % n_q
    k_i = lax.broadcasted_iota(jnp.int32, (grp, PAGE), 1)
    new_mask = k_i <= r_q
    pad = jnp.zeros((PAGE - n_q, HEAD_W), BF16)
    lam = _lam_from(lamv_ref, lam_init)
    for h in range(N_HEADS):
        cols = slice(h * HEAD_W, (h + 1) * HEAD_W)
        qh = q[:, cols]
        qh2 = jnp.concatenate([jnp.where(lane < MAP_W, qh, 0.0), jnp.where(lane >= MAP_W, qh, 0.0)], axis=0).astype(BF16)
        s_pages = [jnp.dot(qh2, kc_refs[pg][0, h].astype(BF16), preferred_element_type=F32) for pg in range(n_pages)]
        k_new = jnp.concatenate([kn_ref[0, :, cols], pad], axis=0)
        s_new = jnp.where(new_mask, lax.dot_general(qh2, k_new, _NT, preferred_element_type=F32), -jnp.inf)
        s = jnp.concatenate(s_pages + [s_new], axis=1)
        m = jnp.max(s, axis=-1, keepdims=True)
        p = jnp.exp(s - m)
        l = jnp.sum(p, axis=-1, keepdims=True)
        pb = p.astype(BF16)
        v_new = jnp.concatenate([vn_ref[0, :, cols], pad], axis=0)
        acc = jnp.dot(pb[:, n_pages * PAGE:], v_new, preferred_element_type=F32)
        for pg in range(n_pages):
            v_h = vc_refs[pg][0, pl.ds(h, PAGE, stride=N_HEADS), :].astype(BF16)
            acc = acc + jnp.dot(pb[:, pg * PAGE:(pg + 1) * PAGE], v_h, preferred_element_type=F32)
        o_all = acc / l
        o = o_all[:n_q] - lam * o_all[n_q:]
        o_ref[0, :, cols] = (_rms(o, subln_ref[...]) * (1.0 - lam_init)).astype(o_ref.dtype)


def _attn_sample(q, cache_kt, cache_v2, page_table, k_new, v_new, lamv, subln, lam_init):
    bsz, n_q, _ = q.shape
    n_pages = page_table.shape[1]
    per_b = lambda b, pt: (b, 0, 0)
    k_specs = [pl.BlockSpec((1, N_HEADS, HEAD_W, PAGE), functools.partial(lambda b, pt, pg: (pt[b, pg], 0, 0, 0), pg=pg))
               for pg in range(n_pages)]
    v_specs = [pl.BlockSpec((1, PAGE * N_HEADS, HEAD_W), functools.partial(lambda b, pt, pg: (pt[b, pg], 0, 0), pg=pg))
               for pg in range(n_pages)]
    grid_spec = pltpu.PrefetchScalarGridSpec(
        num_scalar_prefetch=1,
        grid=(bsz,),
        in_specs=[pl.BlockSpec((1, n_q, D_MODEL), per_b)] + k_specs + v_specs
                 + [pl.BlockSpec((1, n_q, D_MODEL), per_b), pl.BlockSpec((1, n_q, D_MODEL), per_b),
                    pl.BlockSpec(lamv.shape, lambda b, pt: (0, 0)), pl.BlockSpec((1, HEAD_W), lambda b, pt: (0, 0))],
        out_specs=pl.BlockSpec((1, n_q, D_MODEL), per_b),
    )
    return pl.pallas_call(
        functools.partial(_attn_sample_kernel, n_q=n_q, n_pages=n_pages, lam_init=lam_init),
        grid_spec=grid_spec,
        out_shape=jax.ShapeDtypeStruct((bsz, n_q, D_MODEL), F32),
        compiler_params=_params("parallel"),
        name="attn_sample",
    )(page_table, q, *([cache_kt] * n_pages), *([cache_v2] * n_pages), k_new, v_new, lamv, subln)


def _rope_angles(pos):
    inv = ROPE_THETA ** (-jnp.arange(0, ROT_DIM, 2, dtype=F32) / ROT_DIM)
    ang = pos.astype(F32)[:, None] * inv[None, :]
    return jnp.cos(ang), jnp.sin(ang)


def _rope_tables(pos):
    half = ROT_DIM // 2
    cos, sin = _rope_angles(pos)
    n = pos.shape[0]
    ones = jnp.ones((n, MAP_W - ROT_DIM), F32)
    zeros = jnp.zeros((n, MAP_W - ROT_DIM), F32)
    z8 = jnp.zeros((n, half), F32)
    cos_g = jnp.concatenate([cos, cos, ones], axis=1)
    sa_g = jnp.concatenate([-sin, z8, zeros], axis=1)
    sb_g = jnp.concatenate([z8, sin, zeros], axis=1)
    rep = LANES // MAP_W
    return jnp.tile(cos_g, (1, rep)), jnp.tile(sa_g, (1, rep)), jnp.tile(sb_g, (1, rep))


def _pad_lanes(v):
    return jnp.zeros((1, LANES), F32).at[0, :v.shape[0]].set(v.astype(F32))


def _gain_col(g):
    return jnp.broadcast_to(jnp.tile(g.astype(F32), HEAD_W // MAP_W)[:, None], (HEAD_W, LANES))


def kernel(x_prompt, x_sample, cache_k, cache_v, page_table, state_delta, state_conv, ln_mix, ln_mlp, w_up, w_down,
           w_in_a, conv_a, a_log, dt_bias, o_norm_a, w_out_a, ln_kv, w_kv, k_norm, w_q_b, q_norm_b,
           lam_q1, lam_k1, lam_q2, lam_k2, subln_b, w_out_b):
    bsz, seq, _ = x_prompt.shape
    dbsz, dseq, _ = x_sample.shape
    n_pages = page_table.shape[1]
    past_len = n_pages * PAGE
    tp, ts = bsz * seq, dbsz * dseq
    row2 = lambda v: v.reshape(1, -1).astype(F32)

    w_in = w_in_a[0]
    w_main = w_in[:, :CONV_W + D_MODEL].astype(BF16)
    w_ba = jnp.zeros((D_MODEL, 2 * LANES), F32)
    w_ba = w_ba.at[:, :N_HEADS].set(w_in[:, CONV_W + D_MODEL:CONV_W + D_MODEL + N_HEADS])
    w_ba = w_ba.at[:, LANES:LANES + N_HEADS].set(w_in[:, CONV_W + D_MODEL + N_HEADS:]).astype(BF16)
    ln0 = row2(ln_mix[0])
    alog, dtb, onorm = _pad_lanes(a_log[0]), _pad_lanes(dt_bias[0]), row2(o_norm_a[0])

    xp2, xs2 = x_prompt.reshape(tp, D_MODEL), x_sample.reshape(ts, D_MODEL)
    qkv_p, z_p, ba_p = _proj_in(xp2, ln0, w_main, w_ba)
    qkv_s, z_s, ba_s = _proj_in(xs2, ln0, w_main, w_ba)

    nchunk_p = min(4, seq // DELTA_CHUNK)
    o_p, sd_p, ct_p = _delta(qkv_p.reshape(bsz, seq, CONV_W), z_p.reshape(bsz, seq, D_MODEL),
                             ba_p.reshape(bsz, seq, 2 * LANES), conv_a[0], alog, dtb, onorm,
                             chunk=DELTA_CHUNK, n_chunks=nchunk_p, group=min(2, nchunk_p))
    chunk_s = math.gcd(dseq, DELTA_CHUNK)
    cbuf = jnp.concatenate([jnp.zeros((dbsz, SUBLANES - (CONV_TAPS - 1), CONV_W), F32), state_conv[0]], axis=1)
    o_s, sd_s, ct_s = _delta(qkv_s.reshape(dbsz, dseq, CONV_W), z_s.reshape(dbsz, dseq, D_MODEL),
                             ba_s.reshape(dbsz, dseq, 2 * LANES), conv_a[0], alog, dtb, onorm,
                             chunk=chunk_s, n_chunks=dseq // chunk_s, bb=math.gcd(dbsz, 4),
                             s0=state_delta[0], conv_buf=cbuf, out_dtype=F32)

    wo_a, wup0, wdn0 = w_out_a[0].astype(BF16), w_up[0].astype(BF16), w_down[0].astype(BF16)
    h_p = _post(xp2, o_p.reshape(tp, D_MODEL), wo_a, row2(ln_mlp[0]), wup0, wdn0)
    h_s = _post(xs2, o_s.reshape(ts, D_MODEL), wo_a, row2(ln_mlp[0]), wup0, wdn0)

    lam_init = 0.8 - 0.6 * math.exp(-0.3 * 1)
    wkv, wq = w_kv.astype(BF16), w_q_b[0].astype(BF16)
    ln1 = row2(ln_mix[1])
    cos_p, sin_p = _rope_angles(jnp.arange(seq, dtype=jnp.int32))
    kt_p, v_p, kb_p, vt_p, qt_p = _proj_kvq_t(h_p, bsz, row2(ln_kv), ln1, wkv.T, wkv[:, D_MODEL:], wq.T,
                                              _gain_col(k_norm), _gain_col(q_norm_b[0]), cos_p.T, sin_p.T)
    kn = jnp.tile(k_norm.astype(F32), LANES // MAP_W).reshape(1, LANES)
    qn = jnp.tile(q_norm_b[0].astype(F32), LANES // MAP_W).reshape(1, LANES)
    gi = jnp.arange(LANES) // MAP_W
    gmat = (gi[:, None] == gi[None, :]).astype(F32) / MAP_W
    pos_s = past_len + jnp.arange(dseq, dtype=jnp.int32)
    tile_s = min(512, ts) // dseq
    cos_s, sa_s, sb_s = (jnp.tile(t, (tile_s, 1)) for t in _rope_tables(pos_s))
    k_s, v_s, kb_s, vb_s, qb_s = _proj_kvq(h_s, row2(ln_kv), ln1, wkv, wq, kn, qn, gmat, cos_s, sa_s, sb_s)

    lamv = jnp.zeros((SUBLANES, LANES), F32)
    for r, vec in enumerate((lam_q1[0], lam_k1[0], lam_q2[0], lam_k2[0])):
        lamv = lamv.at[r, :MAP_W].set(vec.astype(F32))
    subln = row2(subln_b[0])
    subln_col = jnp.broadcast_to(subln_b[0].astype(F32)[:, None], (HEAD_W, LANES))
    a_p = _attn_prompt(qt_p, kb_p.reshape(bsz, seq, D_MODEL), vt_p, lamv, subln_col, lam_init)
    n_pool = cache_k.shape[0]
    cache_kt = jnp.transpose(cache_k, (0, 2, 3, 4, 1)).reshape(n_pool, N_HEADS, HEAD_W, PAGE)
    cache_v2 = cache_v.reshape(n_pool, PAGE * N_HEADS, HEAD_W)
    a_s = _attn_sample(qb_s.reshape(dbsz, dseq, D_MODEL), cache_kt, cache_v2, page_table,
                       kb_s.reshape(dbsz, dseq, D_MODEL), vb_s.reshape(dbsz, dseq, D_MODEL), lamv, subln, lam_init)

    wo_b, wup1, wdn1 = w_out_b[0].astype(BF16), w_up[1].astype(BF16), w_down[1].astype(BF16)
    y_p = _post(h_p, a_p.reshape(tp, D_MODEL), wo_b, row2(ln_mlp[1]), wup1, wdn1)
    y_s = _post(h_s, a_s.reshape(ts, D_MODEL), wo_b, row2(ln_mlp[1]), wup1, wdn1)

    tail = slice(SUBLANES - (CONV_TAPS - 1), SUBLANES)
    k_prompt = jnp.transpose(kt_p.reshape(bsz, N_HEADS, 2, MAP_W, seq), (0, 4, 1, 2, 3))
    return (y_p.reshape(bsz, seq, D_MODEL), y_s.reshape(dbsz, dseq, D_MODEL),
            k_prompt, v_p.reshape(bsz, seq, N_HEADS, HEAD_W),
            k_s.reshape(dbsz, dseq, N_HEADS, 2, MAP_W), v_s.reshape(dbsz, dseq, N_HEADS, HEAD_W),
            sd_p[None], sd_s[None], ct_p[:, tail][None], ct_s[:, tail][None])
```

```python
import functools
import math

import jax
import jax.numpy as jnp
from jax import lax
from jax.experimental import pallas as pl
from jax.experimental.pallas import tpu as pltpu

F32 = jnp.float32
BF16 = jnp.bfloat16
EPS = 1e-6

D_MODEL = 1024
D_FF = 4 * D_MODEL
N_HEADS = 8
HEAD_W = 128
MAP_W = 64
CONV_TAPS = 4
CONV_W = 3 * D_MODEL
DELTA_CHUNK = 64
ROT_DIM = 16
ROPE_THETA = 500000.0
PAGE = 128
LANES = 128
SUBLANES = 8
ONES_ROWS = 16
LOG2E = math.log2(math.e)
VMEM_LIMIT = 56 * 1024 * 1024

_NT = (((1,), (1,)), ((), ()))
_HI = lax.Precision.HIGHEST


def _dot(a, b):
    return jnp.dot(a.astype(BF16), b.astype(BF16), preferred_element_type=F32)


def _dot_nt(a, b):
    return lax.dot_general(a.astype(BF16), b.astype(BF16), _NT, preferred_element_type=F32)


def _rms(x, g):
    return x * lax.rsqrt(jnp.mean(x * x, axis=-1, keepdims=True) + EPS) * g


def _sigmoid(x):
    return 1.0 / (1.0 + jnp.exp(-x))


def _softplus(x):
    return jnp.maximum(x, 0.0) + jnp.log1p(jnp.exp(-jnp.abs(x)))


def _eye(n, dtype):
    return (lax.broadcasted_iota(jnp.int32, (n, n), 0) == lax.broadcasted_iota(jnp.int32, (n, n), 1)).astype(dtype)


def _const_spec(shape):
    nd = len(shape)
    return pl.BlockSpec(shape, lambda *_: (0,) * nd, pipeline_mode=pl.Buffered(1))


def _params(*sem):
    return pltpu.CompilerParams(dimension_semantics=sem, vmem_limit_bytes=VMEM_LIMIT)


def _row_tile(t, want):
    tm = min(t, want)
    assert t % tm == 0
    return tm


def _proj_in_kernel(x_ref, ln_ref, w_ref, wba_ref, qkv_ref, z_ref, ba_ref):
    h = _rms(x_ref[...], ln_ref[...]).astype(BF16)
    qkv_ref[...] = jnp.dot(h, w_ref[:, :CONV_W], preferred_element_type=F32)
    z_ref[...] = jnp.dot(h, w_ref[:, CONV_W:], preferred_element_type=F32)
    ba_ref[...] = jnp.dot(h, wba_ref[...], preferred_element_type=F32)


def _proj_in(x, ln, w_main, w_ba):
    t = x.shape[0]
    tm = _row_tile(t, 512)
    row = lambda w: pl.BlockSpec((tm, w), lambda i: (i, 0))
    return pl.pallas_call(
        _proj_in_kernel,
        grid=(t // tm,),
        in_specs=[row(D_MODEL), _const_spec((1, D_MODEL)), _const_spec(w_main.shape), _const_spec(w_ba.shape)],
        out_specs=[row(CONV_W), row(D_MODEL), row(2 * LANES)],
        out_shape=[jax.ShapeDtypeStruct((t, CONV_W), F32), jax.ShapeDtypeStruct((t, D_MODEL), F32),
                   jax.ShapeDtypeStruct((t, 2 * LANES), F32)],
        compiler_params=_params("parallel"),
        name="proj_in",
    )(x, ln, w_main, w_ba)


def _delta_kernel(*refs, chunk, n_chunks, bb, group, has_init):
    if has_init:
        (qkv_ref, z_ref, ba_ref, s0_ref, cb_ref, cw_ref, alog_ref, dtb_ref, onorm_ref,
         o_ref, sfin_ref, ctail_ref,
         xp_ref, q_s, k_s, v_s, g_s, beta_s, s_ref, wq_s, ub_s, at_s, kd_s, ee_s) = refs
    else:
        (qkv_ref, z_ref, ba_ref, cw_ref, alog_ref, dtb_ref, onorm_ref,
         o_ref, sfin_ref, ctail_ref,
         xp_ref, q_s, k_s, v_s, g_s, beta_s, s_ref, wq_s, ub_s, at_s, kd_s, ee_s) = refs
    c = chunk
    blk = c * n_chunks
    i = pl.program_id(1)
    last = pl.num_programs(1) - 1
    halo = SUBLANES
    hist = CONV_TAPS - 1

    @pl.when(i == 0)
    def _():
        xp_ref[:, 0:halo, :] = jnp.zeros((bb, halo, CONV_W), F32)
        if has_init:
            s_ref[...] = s0_ref[...]
            xp_ref[:, halo - hist:halo, :] = cb_ref[...]
        else:
            s_ref[...] = jnp.zeros_like(s_ref)

    for bi in range(bb):
        xp_ref[bi, halo:halo + blk, :] = qkv_ref[bi]
        for j in range(CONV_W // LANES):
            cols = slice(j * LANES, (j + 1) * LANES)
            xg = xp_ref[bi, :, cols]
            acc = xg * cw_ref[0:1, cols]
            for w in range(1, CONV_TAPS):
                acc = pltpu.roll(acc, 1, axis=0) + xg * cw_ref[w:w + 1, cols]
            acc = acc[halo:, :]
            y = acc / (1.0 + jnp.exp2(acc * (-LOG2E)))
            if j < 2 * N_HEADS:
                inv_norm = lax.rsqrt(jnp.sum(y * y, axis=-1, keepdims=True) + EPS)
            if j < N_HEADS:
                q_s[bi, :, cols] = y * (inv_norm * (HEAD_W ** -0.5))
            elif j < 2 * N_HEADS:
                y = y * inv_norm
                k_s[bi, :, (j - N_HEADS) * LANES:(j - N_HEADS + 1) * LANES] = y
            else:
                v_s[bi, :, (j - 2 * N_HEADS) * LANES:(j - 2 * N_HEADS + 1) * LANES] = y
        xp_ref[bi, 0:halo, :] = xp_ref[bi, blk:blk + halo, :]

        @pl.when(i == last)
        def _():
            ctail_ref[bi] = xp_ref[bi, halo - hist:halo, :]

        ba = ba_ref[bi]
        beta_s[bi] = _sigmoid(ba[:, :LANES])
        g_s[bi] = -jnp.exp(alog_ref[...]) * _softplus(ba[:, LANES:] + dtb_ref[...])

    row = lax.broadcasted_iota(jnp.int32, (c, c), 0)
    col = lax.broadcasted_iota(jnp.int32, (c, c), 1)
    incl = row >= col
    strict = row > col
    ltri = incl.astype(F32)
    utri = (row <= col).astype(F32)
    eye_c = (row == col).astype(F32)
    eye_f = _eye(LANES, F32)
    eye_b = eye_f.astype(BF16)
    onorm = onorm_ref[...]
    n_sq = int(math.log2(c)) - 1
    heads = range(N_HEADS)
    hcols = [slice(h * LANES, (h + 1) * LANES) for h in heads]

    def phase_a(groups):
        inst = [(gi, h) for gi in range(len(groups)) for h in heads]
        rows, cg, cg_t, beta_c, ecg, ekd, bec = [], [], [], [], [], [], []
        for bi, ci in groups:
            r0 = ci * c if isinstance(ci, int) else pl.multiple_of(ci * c, c)
            rs = pl.ds(r0, c)
            g_c = g_s[bi, rs, :]
            b_c = beta_s[bi, rs, :]
            cg_c = jnp.dot(ltri, g_c, preferred_element_type=F32, precision=_HI)
            g_t = lax.dot_general(eye_f, g_c, _NT, preferred_element_type=F32, precision=_HI)
            cg_t.append(jnp.dot(g_t, utri, preferred_element_type=F32, precision=_HI))
            cg_last = cg_c[c - 1:c, :]
            e_c = jnp.exp(cg_c)
            ee_s[bi, ci] = jnp.exp(cg_last)
            rows.append(rs)
            cg.append(cg_c)
            beta_c.append(b_c)
            ecg.append(e_c)
            ekd.append(jnp.exp(cg_last - cg_c))
            bec.append(b_c * e_c)
        q = [q_s[groups[gi][0], rows[gi], hcols[h]] for gi, h in inst]
        k = [k_s[groups[gi][0], rows[gi], hcols[h]] for gi, h in inst]
        v = [v_s[groups[gi][0], rows[gi], hcols[h]] for gi, h in inst]
        kb = [x.astype(BF16) for x in k]
        kk = [_dot_nt(x, x) for x in kb]
        qk = [_dot_nt(a, b) for a, b in zip(q, kb)]
        beta_h = [beta_c[gi][:, h:h + 1] for gi, h in inst]
        diff = [cg[gi][:, h:h + 1] - cg_t[gi][h:h + 1, :] for gi, h in inst]
        decay = [jnp.where(incl, jnp.exp(jnp.where(incl, d, 0.0)), 0.0) for d in diff]
        nmat = [-(jnp.where(strict, dc * x, 0.0) * b) for dc, x, b in zip(decay, kk, beta_h)]
        pmat = [eye_c + n for n in nmat]
        qmat = [_dot(n, n) for n in nmat]
        for it in range(n_sq):
            if it < n_sq - 1:
                pq = [_dot(jnp.concatenate([p, n], axis=0), n) for p, n in zip(pmat, qmat)]
                pmat = [p + x[:c] for p, x in zip(pmat, pq)]
                qmat = [x[c:] for x in pq]
            else:
                pmat = [p + _dot(p, n) for p, n in zip(pmat, qmat)]
        rhs = [jnp.concatenate([kx * bec[gi][:, h:h + 1], vx * b], axis=1)
               for (gi, h), kx, vx, b in zip(inst, k, v, beta_h)]
        sol = [_dot(p, r) for p, r in zip(pmat, rhs)]
        k_dec = [kx * ekd[gi][:, h:h + 1] for (gi, h), kx in zip(inst, k)]
        kd_t = [lax.dot_general(eye_b, x.astype(BF16), _NT, preferred_element_type=F32) for x in k_dec]
        for n, (gi, h) in enumerate(inst):
            bi, ci = groups[gi]
            q_dec = q[n] * ecg[gi][:, h:h + 1]
            wq_s[bi, ci, h] = jnp.concatenate([sol[n][:, :LANES], q_dec], axis=0).astype(BF16)
            ub_s[bi, ci, h] = sol[n][:, LANES:]
            at_s[bi, ci, h] = decay[n] * qk[n]
            kd_s[bi, ci, h] = kd_t[n]

    def phase_b(ci):
        inst = [(bi, h) for bi in range(bb) for h in heads]
        r0 = ci * c if isinstance(ci, int) else pl.multiple_of(ci * c, c)
        rs = pl.ds(r0, c)
        s_old = [s_ref[bi, h] for bi, h in inst]
        r = [jnp.dot(wq_s[bi, ci, h], s.astype(BF16), preferred_element_type=F32) for (bi, h), s in zip(inst, s_old)]
        u = [ub_s[bi, ci, h] - x[:c] for (bi, h), x in zip(inst, r)]
        ub16 = [x.astype(BF16) for x in u]
        o = [x[c:] + _dot(at_s[bi, ci, h], y) for (bi, h), x, y in zip(inst, r, ub16)]
        for (bi, h), s, y in zip(inst, s_old, ub16):
            s_ref[bi, h] = s * ee_s[bi, ci][:, h:h + 1] + _dot(kd_s[bi, ci, h], y)
        for (bi, h), x in zip(inst, o):
            zt = z_ref[bi, rs, hcols[h]]
            o_ref[bi, rs, hcols[h]] = (_rms(x, onorm) * (zt * _sigmoid(zt))).astype(o_ref.dtype)

    if n_chunks == 1:
        phase_a([(bi, 0) for bi in range(bb)])
        phase_b(0)
    else:
        def a_body(gi, carry):
            phase_a([(bi, gi * group + t) for bi in range(bb) for t in range(group)])
            return carry

        def b_body(ci, carry):
            phase_b(ci)
            return carry

        lax.fori_loop(0, n_chunks // group, a_body, 0)
        lax.fori_loop(0, n_chunks, b_body, 0)

    @pl.when(i == last)
    def _():
        sfin_ref[...] = s_ref[...]


def _delta(qkv, z, ba, conv_w, a_log, dt_bias, o_norm, *, chunk, n_chunks, bb=1, group=1,
           s0=None, conv_buf=None, out_dtype=BF16):
    bsz, length, _ = qkv.shape
    blk = chunk * n_chunks
    assert length % blk == 0 and blk >= SUBLANES and bsz % bb == 0 and n_chunks % group == 0
    has_init = s0 is not None
    seq = lambda w: pl.BlockSpec((bb, blk, w), lambda b, i: (b, i, 0))
    state = pl.BlockSpec((bb, N_HEADS, HEAD_W, HEAD_W), lambda b, i: (b, 0, 0, 0))
    ctail = pl.BlockSpec((bb, CONV_TAPS - 1, CONV_W), lambda b, i: (b, 0, 0))
    in_specs = [seq(CONV_W), seq(D_MODEL), seq(2 * LANES)]
    args = [qkv, z, ba]
    if has_init:
        in_specs += [state, ctail]
        args += [s0, conv_buf]
    in_specs += [_const_spec(conv_w.shape), _const_spec((1, LANES)), _const_spec((1, LANES)), _const_spec((1, HEAD_W))]
    args += [conv_w, a_log, dt_bias, o_norm]
    per = (bb, n_chunks, N_HEADS)
    return pl.pallas_call(
        functools.partial(_delta_kernel, chunk=chunk, n_chunks=n_chunks, bb=bb, group=group, has_init=has_init),
        grid=(bsz // bb, length // blk),
        in_specs=in_specs,
        out_specs=[seq(D_MODEL), state, ctail],
        out_shape=[jax.ShapeDtypeStruct((bsz, length, D_MODEL), out_dtype),
                   jax.ShapeDtypeStruct((bsz, N_HEADS, HEAD_W, HEAD_W), F32),
                   jax.ShapeDtypeStruct((bsz, CONV_TAPS - 1, CONV_W), F32)],
        scratch_shapes=[pltpu.VMEM((bb, blk + SUBLANES, CONV_W), F32),
                        pltpu.VMEM((bb, blk, D_MODEL), F32), pltpu.VMEM((bb, blk, D_MODEL), F32),
                        pltpu.VMEM((bb, blk, D_MODEL), F32),
                        pltpu.VMEM((bb, blk, LANES), F32), pltpu.VMEM((bb, blk, LANES), F32),
                        pltpu.VMEM((bb, N_HEADS, HEAD_W, HEAD_W), F32),
                        pltpu.VMEM(per + (2 * chunk, HEAD_W), BF16), pltpu.VMEM(per + (chunk, HEAD_W), F32),
                        pltpu.VMEM(per + (chunk, chunk), F32), pltpu.VMEM(per + (HEAD_W, chunk), F32),
                        pltpu.VMEM((bb, n_chunks, 1, LANES), F32)],
        compiler_params=_params("parallel", "arbitrary"),
        name="delta_rule",
    )(*args)


def _post_kernel(x_ref, o_ref, wo_ref, ln_ref, wup_ref, wdn_ref, y_ref, *, ff_chunk):
    h1 = x_ref[...] + jnp.dot(o_ref[...].astype(BF16), wo_ref[...], preferred_element_type=F32)
    n = _rms(h1, ln_ref[...]).astype(BF16)
    acc = h1
    for c0 in range(0, D_FF, ff_chunk):
        u = jnp.dot(n, wup_ref[:, c0:c0 + ff_chunk], preferred_element_type=F32)
        a = jnp.square(jnp.maximum(u, 0.0)).astype(BF16)
        acc = acc + jnp.dot(a, wdn_ref[c0:c0 + ff_chunk, :], preferred_element_type=F32)
    y_ref[...] = acc


def _post(x, o, w_out, ln, w_up, w_down):
    t = x.shape[0]
    tm = _row_tile(t, 512)
    row = pl.BlockSpec((tm, D_MODEL), lambda i: (i, 0))
    return pl.pallas_call(
        functools.partial(_post_kernel, ff_chunk=1024),
        grid=(t // tm,),
        in_specs=[row, row, _const_spec(w_out.shape), _const_spec((1, D_MODEL)),
                  _const_spec(w_up.shape), _const_spec(w_down.shape)],
        out_specs=row,
        out_shape=jax.ShapeDtypeStruct((t, D_MODEL), F32),
        compiler_params=_params("parallel"),
        name="post_mlp",
    )(x, o, w_out, ln, w_up, w_down)


def _head_norm_rope(x, gain, gmat, cos, sin_a, sin_b):
    outs = []
    for j in range(D_MODEL // LANES):
        xt = x[:, j * LANES:(j + 1) * LANES]
        ms = jnp.dot(xt * xt, gmat, preferred_element_type=F32, precision=_HI)
        xn = xt * lax.rsqrt(ms + EPS) * gain
        outs.append(xn * cos + pltpu.roll(xn, LANES - ROT_DIM // 2, axis=1) * sin_a
                    + pltpu.roll(xn, ROT_DIM // 2, axis=1) * sin_b)
    return jnp.concatenate(outs, axis=1)


def _proj_kvq_kernel(h_ref, lnkv_ref, lnq_ref, wkv_ref, wq_ref, kn_ref, qn_ref, gmat_ref, cos_ref, sa_ref, sb_ref,
                     k_ref, v_ref, kb_ref, vb_ref, qb_ref):
    h = h_ref[...]
    cos, sin_a, sin_b, gmat = cos_ref[...], sa_ref[...], sb_ref[...], gmat_ref[...]
    nkv = _rms(h, lnkv_ref[...]).astype(BF16)
    k = jnp.dot(nkv, wkv_ref[:, :D_MODEL], preferred_element_type=F32)
    v = jnp.dot(nkv, wkv_ref[:, D_MODEL:], preferred_element_type=F32)
    k = _head_norm_rope(k, kn_ref[...], gmat, cos, sin_a, sin_b)
    k_ref[...] = k
    v_ref[...] = v
    kb_ref[...] = k.astype(BF16)
    vb_ref[...] = v.astype(BF16)
    nq = _rms(h, lnq_ref[...]).astype(BF16)
    q = jnp.dot(nq, wq_ref[...], preferred_element_type=F32)
    q = _head_norm_rope(q, qn_ref[...], gmat, cos, sin_a, sin_b)
    qb_ref[...] = (q * (MAP_W ** -0.5)).astype(BF16)


def _proj_kvq(h, ln_kv, ln_q, w_kv, w_q, k_norm, q_norm, gmat, cos, sin_a, sin_b):
    t = h.shape[0]
    tm = _row_tile(t, min(512, cos.shape[0]))
    n_tab = cos.shape[0] // tm
    row = pl.BlockSpec((tm, D_MODEL), lambda i: (i, 0))
    tab = pl.BlockSpec((tm, LANES), lambda i: (i % n_tab, 0))
    f32_out = jax.ShapeDtypeStruct((t, D_MODEL), F32)
    bf_out = jax.ShapeDtypeStruct((t, D_MODEL), BF16)
    return pl.pallas_call(
        _proj_kvq_kernel,
        grid=(t // tm,),
        in_specs=[row, _const_spec((1, D_MODEL)), _const_spec((1, D_MODEL)), _const_spec(w_kv.shape), _const_spec(w_q.shape),
                  _const_spec((1, LANES)), _const_spec((1, LANES)), _const_spec((LANES, LANES)), tab, tab, tab],
        out_specs=[row] * 5,
        out_shape=[f32_out, f32_out, bf_out, bf_out, bf_out],
        compiler_params=_params("parallel"),
        name="proj_kvq",
    )(h, ln_kv, ln_q, w_kv, w_q, k_norm, q_norm, gmat, cos, sin_a, sin_b)


def _norm_rope_t(x, gain, cos, sin):
    half = ROT_DIM // 2
    outs = []
    for g in range(HEAD_W // MAP_W):
        xg = x[g * MAP_W:(g + 1) * MAP_W, :]
        ms = jnp.mean(xg * xg, axis=0, keepdims=True)
        xn = xg * lax.rsqrt(ms + EPS) * gain[g * MAP_W:(g + 1) * MAP_W, :]
        x1, x2 = xn[0:half], xn[half:ROT_DIM]
        outs += [x1 * cos - x2 * sin, x2 * cos + x1 * sin, xn[ROT_DIM:]]
    return jnp.concatenate(outs, axis=0)


def _proj_kvq_t_kernel(h_ref, lnkv_ref, lnq_ref, wkvt_ref, wv_ref, wqt_ref, kn_ref, qn_ref, cos_ref, sin_ref,
                       kt_ref, v_ref, kb_ref, vt_ref, qt_ref):
    h = h_ref[...]
    tm = h.shape[0]
    hn = h * lax.rsqrt(jnp.mean(h * h, axis=-1, keepdims=True) + EPS)
    nkv = (hn * lnkv_ref[...]).astype(BF16)
    nq = (hn * lnq_ref[...]).astype(BF16)
    cos, sin = cos_ref[...], sin_ref[...]
    rep = tm // LANES
    kn = jnp.concatenate([kn_ref[...]] * rep, axis=1)
    qn = jnp.concatenate([qn_ref[...]] * rep, axis=1)
    kvt = lax.dot_general(wkvt_ref[...], nkv, _NT, preferred_element_type=F32)
    kt = jnp.concatenate([_norm_rope_t(kvt[hd * HEAD_W:(hd + 1) * HEAD_W, :], kn, cos, sin)
                          for hd in range(N_HEADS)], axis=0)
    kt_ref[0] = kt
    vt_ref[0] = kvt[D_MODEL:, :].astype(BF16)
    kb_ref[...] = lax.dot_general(_eye(tm, BF16), kt.astype(BF16), _NT, preferred_element_type=F32).astype(BF16)
    v_ref[...] = jnp.dot(nkv, wv_ref[...], preferred_element_type=F32)
    qt = lax.dot_general(wqt_ref[...], nq, _NT, preferred_element_type=F32)
    qt = jnp.concatenate([_norm_rope_t(qt[hd * HEAD_W:(hd + 1) * HEAD_W, :], qn, cos, sin)
                          for hd in range(N_HEADS)], axis=0)
    qt_ref[0] = (qt * (MAP_W ** -0.5 * LOG2E)).astype(BF16)


def _proj_kvq_t(h, bsz, ln_kv, ln_q, w_kvt, w_v, w_qt, kn_col, qn_col, cos_t, sin_t):
    t = h.shape[0]
    length = t // bsz
    tm = _row_tile(length, 512)
    nl = length // tm
    row = pl.BlockSpec((tm, D_MODEL), lambda b, i: (b * nl + i, 0))
    colb = pl.BlockSpec((1, D_MODEL, tm), lambda b, i: (b, 0, i))
    tab = pl.BlockSpec((ROT_DIM // 2, tm), lambda b, i: (0, i))
    t_f32 = jax.ShapeDtypeStruct((bsz, D_MODEL, length), F32)
    t_bf = jax.ShapeDtypeStruct((bsz, D_MODEL, length), BF16)
    return pl.pallas_call(
        _proj_kvq_t_kernel,
        grid=(bsz, nl),
        in_specs=[row, _const_spec((1, D_MODEL)), _const_spec((1, D_MODEL)), _const_spec(w_kvt.shape),
                  _const_spec(w_v.shape), _const_spec(w_qt.shape), _const_spec((HEAD_W, LANES)),
                  _const_spec((HEAD_W, LANES)), tab, tab],
        out_specs=[colb, row, row, colb, colb],
        out_shape=[t_f32, jax.ShapeDtypeStruct((t, D_MODEL), F32), jax.ShapeDtypeStruct((t, D_MODEL), BF16), t_bf, t_bf],
        compiler_params=_params("parallel", "parallel"),
        name="proj_kvq_t",
    )(h, ln_kv, ln_q, w_kvt, w_v, w_qt, kn_col, qn_col, cos_t, sin_t)


def _lam_from(lamv_ref, lam_init):
    lv = lamv_ref[...]
    s1 = jnp.sum(lv[0:1, :] * lv[1:2, :], axis=-1, keepdims=True)
    s2 = jnp.sum(lv[2:3, :] * lv[3:4, :], axis=-1, keepdims=True)
    return jnp.exp(s1) - jnp.exp(s2) + lam_init


def _attn_prompt_kernel(qt_ref, k_ref, vt_ref, lamv_ref, subln_ref, o_ref, m_ref, acc_ref,
                        s_ref, s1_ref, p_ref, p1_ref, *, bq, strip, lam_init):
    i = pl.program_id(2)
    qt = qt_ref[0]
    row = lax.broadcasted_iota(jnp.int32, (HEAD_W, bq), 0)
    zero = jnp.zeros_like(qt)
    q2t = jnp.concatenate([jnp.where(row < MAP_W, qt, zero), jnp.where(row >= MAP_W, qt, zero)], axis=1)
    strips = [slice(t * strip, (t + 1) * strip) for t in range(2 * bq // strip)]
    m_ref[...] = jnp.full(m_ref.shape, -jnp.inf, F32)
    acc_ref[...] = jnp.zeros_like(acc_ref)
    s_refs, p_refs = (s_ref, s1_ref), (p_ref, p1_ref)
    p1_ref[...] = jnp.zeros(p1_ref.shape, BF16)
    ones = jnp.ones((ONES_ROWS, bq), BF16)

    def keys(j):
        return k_ref[0, pl.ds(pl.multiple_of(j * bq, bq), bq), :]

    def values_t(j):
        vt = vt_ref[0, :, pl.ds(pl.multiple_of(j * bq, bq), bq)]
        return jnp.concatenate([vt, ones], axis=0)

    def scores(j, slot):
        kb = keys(j)
        for sl in strips:
            s_refs[slot][:, sl] = jnp.dot(kb, q2t[:, sl], preferred_element_type=F32)

    def softmax_update(j, cur, masked):
        vt_prev = values_t(jnp.maximum(j - 1, 0))
        for sl in strips:
            pv = jnp.dot(vt_prev, p_refs[1 - cur][:, sl], preferred_element_type=F32)
            s = s_refs[cur][:, sl]
            if masked:
                ki = lax.broadcasted_iota(jnp.int32, (bq, strip), 0)
                qi = (lax.broadcasted_iota(jnp.int32, (bq, strip), 1) + sl.start) % bq
                s = jnp.where(ki <= qi, s, -jnp.inf)
            m_old = m_ref[:, sl]
            m_new = jnp.maximum(m_old, jnp.max(s, axis=0, keepdims=True))
            alpha = jnp.exp2(m_old - m_new)
            pb = jnp.exp2(s - m_new).astype(BF16)
            acc = alpha * (acc_ref[:, sl] + pv)
            if masked:
                acc = acc + jnp.dot(values_t(j), pb, preferred_element_type=F32)
            else:
                p_refs[cur][:, sl] = pb
            acc_ref[:, sl] = acc
            m_ref[:, sl] = m_new

    def sub_step(j, cur):
        scores(j + 1, 1 - cur)
        softmax_update(j, cur, False)

    scores(0, 0)

    def body(t, carry):
        sub_step(2 * t, 0)
        sub_step(2 * t + 1, 1)
        return carry

    lax.fori_loop(0, i // 2, body, 0)
    odd = i % 2 == 1

    @pl.when(odd)
    def _():
        sub_step(i - 1, 0)
        softmax_update(i, 1, True)

    @pl.when(jnp.logical_not(odd))
    def _():
        softmax_update(i, 0, True)

    lam = _lam_from(lamv_ref, lam_init)
    o_all = acc_ref[0:HEAD_W, :] / acc_ref[HEAD_W:HEAD_W + 1, :]
    ot = o_all[:, :bq] - lam * o_all[:, bq:]
    gain = jnp.concatenate([subln_ref[...]] * (bq // LANES), axis=1)
    ot = ot * lax.rsqrt(jnp.mean(ot * ot, axis=0, keepdims=True) + EPS) * gain * (1.0 - lam_init)
    o = lax.dot_general(_eye(bq, BF16), ot.astype(BF16), _NT, preferred_element_type=F32)
    o_ref[0] = o.astype(o_ref.dtype)


def _attn_prompt(qt, k, vt, lamv, subln_col, lam_init):
    bsz, length, _ = k.shape
    bq = min(512, length)
    assert length % bq == 0
    return pl.pallas_call(
        functools.partial(_attn_prompt_kernel, bq=bq, strip=min(512, bq), lam_init=lam_init),
        grid=(bsz, N_HEADS, length // bq),
        in_specs=[pl.BlockSpec((1, HEAD_W, bq), lambda b, h, i: (b, h, i)),
                  pl.BlockSpec((1, length, HEAD_W), lambda b, h, i: (b, 0, h)),
                  pl.BlockSpec((1, HEAD_W, length), lambda b, h, i: (b, h, 0)),
                  _const_spec(lamv.shape), _const_spec((HEAD_W, LANES))],
        out_specs=pl.BlockSpec((1, bq, HEAD_W), lambda b, h, i: (b, i, h)),
        out_shape=jax.ShapeDtypeStruct((bsz, length, D_MODEL), BF16),
        scratch_shapes=[pltpu.VMEM((1, 2 * bq), F32), pltpu.VMEM((HEAD_W + ONES_ROWS, 2 * bq), F32),
                        pltpu.VMEM((bq, 2 * bq), F32), pltpu.VMEM((bq, 2 * bq), F32),
                        pltpu.VMEM((bq, 2 * bq), BF16), pltpu.VMEM((bq, 2 * bq), BF16)],
        compiler_params=_params("parallel", "parallel", "arbitrary"),
        name="attn_prompt",
    )(qt, k, vt, lamv, subln_col)


def _attn_sample_kernel(pt_ref, q_ref, *refs, n_q, n_pages, lam_init):
    del pt_ref
    kc_refs = refs[:n_pages]
    vc_refs = refs[n_pages:2 * n_pages]
    kn_ref, vn_ref, lamv_ref, subln_ref, o_ref = refs[2 * n_pages:]
    grp = 2 * n_q
    q = q_ref[0].astype(F32)
    lane = lax.broadcasted_iota(jnp.int32, (n_q, HEAD_W), 1)
    r_q = lax.broadcasted_iota(jnp.int32, (grp, PAGE), 0) % n_q
    k_i = lax.broadcasted_iota(jnp.int32, (grp, PAGE), 1)
    new_mask = k_i <= r_q
    pad = jnp.zeros((PAGE - n_q, HEAD_W), BF16)
    lam = _lam_from(lamv_ref, lam_init)
    for h in range(N_HEADS):
        cols = slice(h * HEAD_W, (h + 1) * HEAD_W)
        qh = q[:, cols]
        qh2 = jnp.concatenate([jnp.where(lane < MAP_W, qh, 0.0), jnp.where(lane >= MAP_W, qh, 0.0)], axis=0).astype(BF16)
        s_pages = [jnp.dot(qh2, kc_refs[pg][0, h].astype(BF16), preferred_element_type=F32) for pg in range(n_pages)]
        k_new = jnp.concatenate([kn_ref[0, :, cols], pad], axis=0)
        s_new = jnp.where(new_mask, lax.dot_general(qh2, k_new, _NT, preferred_element_type=F32), -jnp.inf)
        s = jnp.concatenate(s_pages + [s_new], axis=1)
        m = jnp.max(s, axis=-1, keepdims=True)
        p = jnp.exp(s - m)
        l = jnp.sum(p, axis=-1, keepdims=True)
        pb = p.astype(BF16)
        v_new = jnp.concatenate([vn_ref[0, :, cols], pad], axis=0)
        acc = jnp.dot(pb[:, n_pages * PAGE:], v_new, preferred_element_type=F32)
        for pg in range(n_pages):
            v_h = vc_refs[pg][0, pl.ds(h, PAGE, stride=N_HEADS), :].astype(BF16)
            acc = acc + jnp.dot(pb[:, pg * PAGE:(pg + 1) * PAGE], v_h, preferred_element_type=F32)
        o_all = acc / l
        o = o_all[:n_q] - lam * o_all[n_q:]
        o_ref[0, :, cols] = (_rms(o, subln_ref[...]) * (1.0 - lam_init)).astype(o_ref.dtype)


def _attn_sample(q, cache_kt, cache_v2, page_table, k_new, v_new, lamv, subln, lam_init):
    bsz, n_q, _ = q.shape
    n_pages = page_table.shape[1]
    per_b = lambda b, pt: (b, 0, 0)
    k_specs = [pl.BlockSpec((1, N_HEADS, HEAD_W, PAGE), functools.partial(lambda b, pt, pg: (pt[b, pg], 0, 0, 0), pg=pg))
               for pg in range(n_pages)]
    v_specs = [pl.BlockSpec((1, PAGE * N_HEADS, HEAD_W), functools.partial(lambda b, pt, pg: (pt[b, pg], 0, 0), pg=pg))
               for pg in range(n_pages)]
    grid_spec = pltpu.PrefetchScalarGridSpec(
        num_scalar_prefetch=1,
        grid=(bsz,),
        in_specs=[pl.BlockSpec((1, n_q, D_MODEL), per_b)] + k_specs + v_specs
                 + [pl.BlockSpec((1, n_q, D_MODEL), per_b), pl.BlockSpec((1, n_q, D_MODEL), per_b),
                    pl.BlockSpec(lamv.shape, lambda b, pt: (0, 0)), pl.BlockSpec((1, HEAD_W), lambda b, pt: (0, 0))],
        out_specs=pl.BlockSpec((1, n_q, D_MODEL), per_b),
    )
    return pl.pallas_call(
        functools.partial(_attn_sample_kernel, n_q=n_q, n_pages=n_pages, lam_init=lam_init),
        grid_spec=grid_spec,
        out_shape=jax.ShapeDtypeStruct((bsz, n_q, D_MODEL), F32),
        compiler_params=_params("parallel"),
        name="attn_sample",
    )(page_table, q, *([cache_kt] * n_pages), *([cache_v2] * n_pages), k_new, v_new, lamv, subln)


def _rope_angles(pos):
    inv = ROPE_THETA ** (-jnp.arange(0, ROT_DIM, 2, dtype=F32) / ROT_DIM)
    ang = pos.astype(F32)[:, None] * inv[None, :]
    return jnp.cos(ang), jnp.sin(ang)


def _rope_tables(pos):
    half = ROT_DIM // 2
    cos, sin = _rope_angles(pos)
    n = pos.shape[0]
    ones = jnp.ones((n, MAP_W - ROT_DIM), F32)
    zeros = jnp.zeros((n, MAP_W - ROT_DIM), F32)
    z8 = jnp.zeros((n, half), F32)
    cos_g = jnp.concatenate([cos, cos, ones], axis=1)
    sa_g = jnp.concatenate([-sin, z8, zeros], axis=1)
    sb_g = jnp.concatenate([z8, sin, zeros], axis=1)
    rep = LANES // MAP_W
    return jnp.tile(cos_g, (1, rep)), jnp.tile(sa_g, (1, rep)), jnp.tile(sb_g, (1, rep))


def _pad_lanes(v):
    return jnp.zeros((1, LANES), F32).at[0, :v.shape[0]].set(v.astype(F32))


def _gain_col(g):
    return jnp.broadcast_to(jnp.tile(g.astype(F32), HEAD_W // MAP_W)[:, None], (HEAD_W, LANES))


def kernel(x_prompt, x_sample, cache_k, cache_v, page_table, state_delta, state_conv, ln_mix, ln_mlp, w_up, w_down,
           w_in_a, conv_a, a_log, dt_bias, o_norm_a, w_out_a, ln_kv, w_kv, k_norm, w_q_b, q_norm_b,
           lam_q1, lam_k1, lam_q2, lam_k2, subln_b, w_out_b):
    bsz, seq, _ = x_prompt.shape
    dbsz, dseq, _ = x_sample.shape
    n_pages = page_table.shape[1]
    past_len = n_pages * PAGE
    tp, ts = bsz * seq, dbsz * dseq
    row2 = lambda v: v.reshape(1, -1).astype(F32)

    w_in = w_in_a[0]
    w_main = w_in[:, :CONV_W + D_MODEL].astype(BF16)
    w_ba = jnp.zeros((D_MODEL, 2 * LANES), F32)
    w_ba = w_ba.at[:, :N_HEADS].set(w_in[:, CONV_W + D_MODEL:CONV_W + D_MODEL + N_HEADS])
    w_ba = w_ba.at[:, LANES:LANES + N_HEADS].set(w_in[:, CONV_W + D_MODEL + N_HEADS:]).astype(BF16)
    ln0 = row2(ln_mix[0])
    alog, dtb, onorm = _pad_lanes(a_log[0]), _pad_lanes(dt_bias[0]), row2(o_norm_a[0])

    xp2, xs2 = x_prompt.reshape(tp, D_MODEL), x_sample.reshape(ts, D_MODEL)
    qkv_p, z_p, ba_p = _proj_in(xp2, ln0, w_main, w_ba)
    qkv_s, z_s, ba_s = _proj_in(xs2, ln0, w_main, w_ba)

    nchunk_p = min(2, seq // DELTA_CHUNK)
    o_p, sd_p, ct_p = _delta(qkv_p.reshape(bsz, seq, CONV_W), z_p.reshape(bsz, seq, D_MODEL),
                             ba_p.reshape(bsz, seq, 2 * LANES), conv_a[0], alog, dtb, onorm,
                             chunk=DELTA_CHUNK, n_chunks=nchunk_p, group=1, bb=math.gcd(bsz, 2))
    chunk_s = math.gcd(dseq, DELTA_CHUNK)
    o_s, sd_s, ct_s = _delta(qkv_s.reshape(dbsz, dseq, CONV_W), z_s.reshape(dbsz, dseq, D_MODEL),
                             ba_s.reshape(dbsz, dseq, 2 * LANES), conv_a[0], alog, dtb, onorm,
                             chunk=chunk_s, n_chunks=dseq // chunk_s, bb=math.gcd(dbsz, 4),
                             s0=state_delta[0], conv_buf=state_conv[0], out_dtype=F32)

    wo_a, wup0, wdn0 = w_out_a[0].astype(BF16), w_up[0].astype(BF16), w_down[0].astype(BF16)
    h_p = _post(xp2, o_p.reshape(tp, D_MODEL), wo_a, row2(ln_mlp[0]), wup0, wdn0)
    h_s = _post(xs2, o_s.reshape(ts, D_MODEL), wo_a, row2(ln_mlp[0]), wup0, wdn0)

    lam_init = 0.8 - 0.6 * math.exp(-0.3 * 1)
    wkv, wq = w_kv.astype(BF16), w_q_b[0].astype(BF16)
    ln1 = row2(ln_mix[1])
    cos_p, sin_p = _rope_angles(jnp.arange(seq, dtype=jnp.int32))
    kt_p, v_p, kb_p, vt_p, qt_p = _proj_kvq_t(h_p, bsz, row2(ln_kv), ln1, wkv.T, wkv[:, D_MODEL:], wq.T,
                                              _gain_col(k_norm), _gain_col(q_norm_b[0]), cos_p.T, sin_p.T)
    kn = jnp.tile(k_norm.astype(F32), LANES // MAP_W).reshape(1, LANES)
    qn = jnp.tile(q_norm_b[0].astype(F32), LANES // MAP_W).reshape(1, LANES)
    gi = jnp.arange(LANES) // MAP_W
    gmat = (gi[:, None] == gi[None, :]).astype(F32) / MAP_W
    pos_s = past_len + jnp.arange(dseq, dtype=jnp.int32)
    tile_s = min(512, ts) // dseq
    cos_s, sa_s, sb_s = (jnp.tile(t, (tile_s, 1)) for t in _rope_tables(pos_s))
    k_s, v_s, kb_s, vb_s, qb_s = _proj_kvq(h_s, row2(ln_kv), ln1, wkv, wq, kn, qn, gmat, cos_s, sa_s, sb_s)

    lamv = jnp.zeros((SUBLANES, LANES), F32)
    for r, vec in enumerate((lam_q1[0], lam_k1[0], lam_q2[0], lam_k2[0])):
        lamv = lamv.at[r, :MAP_W].set(vec.astype(F32))
    subln = row2(subln_b[0])
    subln_col = jnp.broadcast_to(subln_b[0].astype(F32)[:, None], (HEAD_W, LANES))
    n_pool = cache_k.shape[0]
    cache_kt = jnp.transpose(cache_k, (0, 2, 3, 4, 1)).reshape(n_pool, N_HEADS, HEAD_W, PAGE)
    cache_v2 = cache_v.reshape(n_pool, PAGE * N_HEADS, HEAD_W)
    a_p = _attn_prompt(qt_p, kb_p.reshape(bsz, seq, D_MODEL), vt_p, lamv, subln_col, lam_init)
    a_s = _attn_sample(qb_s.reshape(dbsz, dseq, D_MODEL), cache_kt, cache_v2, page_table,
                       kb_s.reshape(dbsz, dseq, D_MODEL), vb_s.reshape(dbsz, dseq, D_MODEL), lamv, subln, lam_init)

    wo_b, wup1, wdn1 = w_out_b[0].astype(BF16), w_up[1].astype(BF16), w_down[1].astype(BF16)
    y_p = _post(h_p, a_p.reshape(tp, D_MODEL), wo_b, row2(ln_mlp[1]), wup1, wdn1)
    y_s = _post(h_s, a_s.reshape(ts, D_MODEL), wo_b, row2(ln_mlp[1]), wup1, wdn1)

    k_prompt = jnp.transpose(kt_p.reshape(bsz, N_HEADS, 2, MAP_W, seq), (0, 4, 1, 2, 3))
    return (y_p.reshape(bsz, seq, D_MODEL), y_s.reshape(dbsz, dseq, D_MODEL),
            k_prompt, v_p.reshape(bsz, seq, N_HEADS, HEAD_W),
            k_s.reshape(dbsz, dseq, N_HEADS, 2, MAP_W), v_s.reshape(dbsz, dseq, N_HEADS, HEAD_W),
            sd_p[None], sd_s[None], ct_p[None], ct_s[None])
```

```python
import functools
import math

import jax
import jax.numpy as jnp
from jax import lax
from jax.experimental import pallas as pl
from jax.experimental.pallas import tpu as pltpu

F32 = jnp.float32
BF16 = jnp.bfloat16
EPS = 1e-6

D_MODEL = 1024
D_FF = 4 * D_MODEL
N_HEADS = 8
HEAD_W = 128
MAP_W = 64
CONV_TAPS = 4
CONV_W = 3 * D_MODEL
DELTA_CHUNK = 64
ROT_DIM = 16
ROPE_THETA = 500000.0
PAGE = 128
LANES = 128
SUBLANES = 8
ONES_ROWS = 16
LOG2E = math.log2(math.e)
VMEM_LIMIT = 56 * 1024 * 1024

_NT = (((1,), (1,)), ((), ()))
_HI = lax.Precision.HIGHEST


def _dot(a, b):
    return jnp.dot(a.astype(BF16), b.astype(BF16), preferred_element_type=F32)


def _dot_nt(a, b):
    return lax.dot_general(a.astype(BF16), b.astype(BF16), _NT, preferred_element_type=F32)


def _rms(x, g):
    return x * lax.rsqrt(jnp.mean(x * x, axis=-1, keepdims=True) + EPS) * g


def _sigmoid(x):
    return 1.0 / (1.0 + jnp.exp(-x))


def _softplus(x):
    return jnp.maximum(x, 0.0) + jnp.log1p(jnp.exp(-jnp.abs(x)))


def _const_spec(shape):
    nd = len(shape)
    return pl.BlockSpec(shape, lambda *_: (0,) * nd, pipeline_mode=pl.Buffered(1))


def _params(*sem):
    return pltpu.CompilerParams(dimension_semantics=sem, vmem_limit_bytes=VMEM_LIMIT)


def _row_tile(t, want):
    tm = min(t, want)
    assert t % tm == 0
    return tm


def _proj_in_kernel(x_ref, ln_ref, w_ref, wba_ref, qkv_ref, z_ref, ba_ref):
    h = _rms(x_ref[...], ln_ref[...]).astype(BF16)
    qkv_ref[...] = jnp.dot(h, w_ref[:, :CONV_W], preferred_element_type=F32)
    z_ref[...] = jnp.dot(h, w_ref[:, CONV_W:], preferred_element_type=F32)
    ba_ref[...] = jnp.dot(h, wba_ref[...], preferred_element_type=F32)


def _proj_in(x, ln, w_main, w_ba):
    t = x.shape[0]
    tm = _row_tile(t, 512)
    row = lambda w: pl.BlockSpec((tm, w), lambda i: (i, 0))
    return pl.pallas_call(
        _proj_in_kernel,
        grid=(t // tm,),
        in_specs=[row(D_MODEL), _const_spec((1, D_MODEL)), _const_spec(w_main.shape), _const_spec(w_ba.shape)],
        out_specs=[row(CONV_W), row(D_MODEL), row(2 * LANES)],
        out_shape=[jax.ShapeDtypeStruct((t, CONV_W), F32), jax.ShapeDtypeStruct((t, D_MODEL), F32),
                   jax.ShapeDtypeStruct((t, 2 * LANES), F32)],
        compiler_params=_params("parallel"),
        name="proj_in",
    )(x, ln, w_main, w_ba)


def _delta_kernel(*refs, chunk, n_chunks, bb, group, has_init):
    if has_init:
        (qkv_ref, z_ref, ba_ref, s0_ref, cb_ref, cw_ref, alog_ref, dtb_ref, onorm_ref,
         o_ref, sfin_ref, ctail_ref,
         xp_ref, q_s, k_s, v_s, g_s, beta_s, s_ref, wq_s, ub_s, at_s, kd_s, ee_s) = refs
    else:
        (qkv_ref, z_ref, ba_ref, cw_ref, alog_ref, dtb_ref, onorm_ref,
         o_ref, sfin_ref, ctail_ref,
         xp_ref, q_s, k_s, v_s, g_s, beta_s, s_ref, wq_s, ub_s, at_s, kd_s, ee_s) = refs
    c = chunk
    blk = c * n_chunks
    i = pl.program_id(1)
    last = pl.num_programs(1) - 1
    halo = SUBLANES
    hist = CONV_TAPS - 1

    @pl.when(i == 0)
    def _():
        xp_ref[:, 0:halo, :] = jnp.zeros((bb, halo, CONV_W), F32)
        if has_init:
            s_ref[...] = s0_ref[...]
            xp_ref[:, halo - hist:halo, :] = cb_ref[...]
        else:
            s_ref[...] = jnp.zeros_like(s_ref)

    for bi in range(bb):
        xp_ref[bi, halo:halo + blk, :] = qkv_ref[bi]
        for j in range(CONV_W // LANES):
            cols = slice(j * LANES, (j + 1) * LANES)
            xg = xp_ref[bi, :, cols]
            acc = xg * cw_ref[0:1, cols]
            for w in range(1, CONV_TAPS):
                acc = pltpu.roll(acc, 1, axis=0) + xg * cw_ref[w:w + 1, cols]
            acc = acc[halo:, :]
            y = acc / (1.0 + jnp.exp2(acc * (-LOG2E)))
            if j < 2 * N_HEADS:
                inv_norm = lax.rsqrt(jnp.sum(y * y, axis=-1, keepdims=True) + EPS)
            if j < N_HEADS:
                q_s[bi, :, cols] = y * (inv_norm * (HEAD_W ** -0.5))
            elif j < 2 * N_HEADS:
                y = y * inv_norm
                k_s[bi, :, (j - N_HEADS) * LANES:(j - N_HEADS + 1) * LANES] = y
            else:
                v_s[bi, :, (j - 2 * N_HEADS) * LANES:(j - 2 * N_HEADS + 1) * LANES] = y
        xp_ref[bi, 0:halo, :] = xp_ref[bi, blk:blk + halo, :]

        @pl.when(i == last)
        def _():
            ctail_ref[bi] = xp_ref[bi, halo - hist:halo, :]

        ba = ba_ref[bi]
        beta_s[bi] = _sigmoid(ba[:, :LANES])
        g_s[bi] = -jnp.exp(alog_ref[...]) * _softplus(ba[:, LANES:] + dtb_ref[...])

    row = lax.broadcasted_iota(jnp.int32, (c, c), 0)
    col = lax.broadcasted_iota(jnp.int32, (c, c), 1)
    incl = row >= col
    strict = row > col
    ltri = incl.astype(F32)
    eye_c = (row == col).astype(F32)
    onorm = onorm_ref[...]
    n_sq = int(math.log2(c)) - 1
    heads = range(N_HEADS)
    hcols = [slice(h * LANES, (h + 1) * LANES) for h in heads]

    def phase_a(groups):
        inst = [(gi, h) for gi in range(len(groups)) for h in heads]
        rows, cg, cg_t, beta_c, ecg, ekd, bec = [], [], [], [], [], [], []
        for bi, ci in groups:
            r0 = ci * c if isinstance(ci, int) else pl.multiple_of(ci * c, c)
            rs = pl.ds(r0, c)
            g_c = g_s[bi, rs, :]
            b_c = beta_s[bi, rs, :]
            cg_c = jnp.dot(ltri, g_c, preferred_element_type=F32, precision=_HI)
            cg_t.append(cg_c.T)
            cg_last = cg_c[c - 1:c, :]
            e_c = jnp.exp(cg_c)
            ee_s[bi, ci] = jnp.exp(cg_last)
            rows.append(rs)
            cg.append(cg_c)
            beta_c.append(b_c)
            ecg.append(e_c)
            ekd.append(jnp.exp(cg_last - cg_c))
            bec.append(b_c * e_c)
        q = [q_s[groups[gi][0], rows[gi], hcols[h]] for gi, h in inst]
        k = [k_s[groups[gi][0], rows[gi], hcols[h]] for gi, h in inst]
        v = [v_s[groups[gi][0], rows[gi], hcols[h]] for gi, h in inst]
        kb = [x.astype(BF16) for x in k]
        kk = [_dot_nt(x, x) for x in kb]
        qk = [_dot_nt(a, b) for a, b in zip(q, kb)]
        beta_h = [beta_c[gi][:, h:h + 1] for gi, h in inst]
        diff = [cg[gi][:, h:h + 1] - cg_t[gi][h:h + 1, :] for gi, h in inst]
        decay = [jnp.where(incl, jnp.exp(jnp.where(incl, d, 0.0)), 0.0) for d in diff]
        nmat = [-(jnp.where(strict, dc * x, 0.0) * b) for dc, x, b in zip(decay, kk, beta_h)]
        pmat = [eye_c + n for n in nmat]
        qmat = [_dot(n, n) for n in nmat]
        for it in range(n_sq):
            if it < n_sq - 1:
                pq = [_dot(jnp.concatenate([p, n], axis=0), n) for p, n in zip(pmat, qmat)]
                pmat = [p + x[:c] for p, x in zip(pmat, pq)]
                qmat = [x[c:] for x in pq]
            else:
                pmat = [p + _dot(p, n) for p, n in zip(pmat, qmat)]
        rhs = [jnp.concatenate([kx * bec[gi][:, h:h + 1], vx * b], axis=1)
               for (gi, h), kx, vx, b in zip(inst, k, v, beta_h)]
        sol = [_dot(p, r) for p, r in zip(pmat, rhs)]
        k_dec = [kx * ekd[gi][:, h:h + 1] for (gi, h), kx in zip(inst, k)]
        kd_t = [x.T for x in k_dec]
        for n, (gi, h) in enumerate(inst):
            bi, ci = groups[gi]
            q_dec = q[n] * ecg[gi][:, h:h + 1]
            wq_s[bi, ci, h] = jnp.concatenate([sol[n][:, :LANES], q_dec], axis=0).astype(BF16)
            ub_s[bi, ci, h] = sol[n][:, LANES:]
            at_s[bi, ci, h] = decay[n] * qk[n]
            kd_s[bi, ci, h] = kd_t[n]

    def phase_b(ci):
        inst = [(bi, h) for bi in range(bb) for h in heads]
        r0 = ci * c if isinstance(ci, int) else pl.multiple_of(ci * c, c)
        rs = pl.ds(r0, c)
        s_old = [s_ref[bi, h] for bi, h in inst]
        r = [jnp.dot(wq_s[bi, ci, h], s.astype(BF16), preferred_element_type=F32) for (bi, h), s in zip(inst, s_old)]
        u = [ub_s[bi, ci, h] - x[:c] for (bi, h), x in zip(inst, r)]
        ub16 = [x.astype(BF16) for x in u]
        o = [x[c:] + _dot(at_s[bi, ci, h], y) for (bi, h), x, y in zip(inst, r, ub16)]
        for (bi, h), s, y in zip(inst, s_old, ub16):
            s_ref[bi, h] = s * ee_s[bi, ci][:, h:h + 1] + _dot(kd_s[bi, ci, h], y)
        for (bi, h), x in zip(inst, o):
            zt = z_ref[bi, rs, hcols[h]]
            o_ref[bi, rs, hcols[h]] = (_rms(x, onorm) * (zt * _sigmoid(zt))).astype(o_ref.dtype)

    if n_chunks == 1:
        phase_a([(bi, 0) for bi in range(bb)])
        phase_b(0)
    else:
        def a_body(gi, carry):
            phase_a([(bi, gi * group + t) for bi in range(bb) for t in range(group)])
            return carry

        def b_body(ci, carry):
            phase_b(ci)
            return carry

        lax.fori_loop(0, n_chunks // group, a_body, 0)
        lax.fori_loop(0, n_chunks, b_body, 0)

    @pl.when(i == last)
    def _():
        sfin_ref[...] = s_ref[...]


def _delta(qkv, z, ba, conv_w, a_log, dt_bias, o_norm, *, chunk, n_chunks, bb=1, group=1,
           s0=None, conv_buf=None, out_dtype=BF16):
    bsz, length, _ = qkv.shape
    blk = chunk * n_chunks
    assert length % blk == 0 and blk >= SUBLANES and bsz % bb == 0 and n_chunks % group == 0
    has_init = s0 is not None
    seq = lambda w: pl.BlockSpec((bb, blk, w), lambda b, i: (b, i, 0))
    state = pl.BlockSpec((bb, N_HEADS, HEAD_W, HEAD_W), lambda b, i: (b, 0, 0, 0))
    ctail = pl.BlockSpec((bb, CONV_TAPS - 1, CONV_W), lambda b, i: (b, 0, 0))
    in_specs = [seq(CONV_W), seq(D_MODEL), seq(2 * LANES)]
    args = [qkv, z, ba]
    if has_init:
        in_specs += [state, ctail]
        args += [s0, conv_buf]
    in_specs += [_const_spec(conv_w.shape), _const_spec((1, LANES)), _const_spec((1, LANES)), _const_spec((1, HEAD_W))]
    args += [conv_w, a_log, dt_bias, o_norm]
    per = (bb, n_chunks, N_HEADS)
    return pl.pallas_call(
        functools.partial(_delta_kernel, chunk=chunk, n_chunks=n_chunks, bb=bb, group=group, has_init=has_init),
        grid=(bsz // bb, length // blk),
        in_specs=in_specs,
        out_specs=[seq(D_MODEL), state, ctail],
        out_shape=[jax.ShapeDtypeStruct((bsz, length, D_MODEL), out_dtype),
                   jax.ShapeDtypeStruct((bsz, N_HEADS, HEAD_W, HEAD_W), F32),
                   jax.ShapeDtypeStruct((bsz, CONV_TAPS - 1, CONV_W), F32)],
        scratch_shapes=[pltpu.VMEM((bb, blk + SUBLANES, CONV_W), F32),
                        pltpu.VMEM((bb, blk, D_MODEL), F32), pltpu.VMEM((bb, blk, D_MODEL), F32),
                        pltpu.VMEM((bb, blk, D_MODEL), F32),
                        pltpu.VMEM((bb, blk, LANES), F32), pltpu.VMEM((bb, blk, LANES), F32),
                        pltpu.VMEM((bb, N_HEADS, HEAD_W, HEAD_W), F32),
                        pltpu.VMEM(per + (2 * chunk, HEAD_W), BF16), pltpu.VMEM(per + (chunk, HEAD_W), F32),
                        pltpu.VMEM(per + (chunk, chunk), F32), pltpu.VMEM(per + (HEAD_W, chunk), F32),
                        pltpu.VMEM((bb, n_chunks, 1, LANES), F32)],
        compiler_params=_params("parallel", "arbitrary"),
        name="delta_rule",
    )(*args)


def _post_kernel(x_ref, o_ref, wo_ref, ln_ref, wup_ref, wdn_ref, y_ref, *, ff_chunk):
    h1 = x_ref[...] + jnp.dot(o_ref[...].astype(BF16), wo_ref[...], preferred_element_type=F32)
    n = _rms(h1, ln_ref[...]).astype(BF16)
    acc = h1
    for c0 in range(0, D_FF, ff_chunk):
        u = jnp.dot(n, wup_ref[:, c0:c0 + ff_chunk], preferred_element_type=F32)
        a = jnp.square(jnp.maximum(u, 0.0)).astype(BF16)
        acc = acc + jnp.dot(a, wdn_ref[c0:c0 + ff_chunk, :], preferred_element_type=F32)
    y_ref[...] = acc


def _post(x, o, w_out, ln, w_up, w_down):
    t = x.shape[0]
    tm = _row_tile(t, 512)
    row = pl.BlockSpec((tm, D_MODEL), lambda i: (i, 0))
    return pl.pallas_call(
        functools.partial(_post_kernel, ff_chunk=1024),
        grid=(t // tm,),
        in_specs=[row, row, _const_spec(w_out.shape), _const_spec((1, D_MODEL)),
                  _const_spec(w_up.shape), _const_spec(w_down.shape)],
        out_specs=row,
        out_shape=jax.ShapeDtypeStruct((t, D_MODEL), F32),
        compiler_params=_params("parallel"),
        name="post_mlp",
    )(x, o, w_out, ln, w_up, w_down)


def _head_norm_rope(x, gain, gmat, cos, sin_a, sin_b):
    outs = []
    for j in range(D_MODEL // LANES):
        xt = x[:, j * LANES:(j + 1) * LANES]
        ms = jnp.dot(xt * xt, gmat, preferred_element_type=F32, precision=_HI)
        xn = xt * lax.rsqrt(ms + EPS) * gain
        outs.append(xn * cos + pltpu.roll(xn, LANES - ROT_DIM // 2, axis=1) * sin_a
                    + pltpu.roll(xn, ROT_DIM // 2, axis=1) * sin_b)
    return jnp.concatenate(outs, axis=1)


def _proj_kvq_kernel(h_ref, lnkv_ref, lnq_ref, wkv_ref, wq_ref, kn_ref, qn_ref, gmat_ref, cos_ref, sa_ref, sb_ref,
                     k_ref, v_ref, kb_ref, vb_ref, qb_ref):
    h = h_ref[...]
    cos, sin_a, sin_b, gmat = cos_ref[...], sa_ref[...], sb_ref[...], gmat_ref[...]
    nkv = _rms(h, lnkv_ref[...]).astype(BF16)
    k = jnp.dot(nkv, wkv_ref[:, :D_MODEL], preferred_element_type=F32)
    v = jnp.dot(nkv, wkv_ref[:, D_MODEL:], preferred_element_type=F32)
    k = _head_norm_rope(k, kn_ref[...], gmat, cos, sin_a, sin_b)
    k_ref[...] = k
    v_ref[...] = v
    kb_ref[...] = k.astype(BF16)
    vb_ref[...] = v.astype(BF16)
    nq = _rms(h, lnq_ref[...]).astype(BF16)
    q = jnp.dot(nq, wq_ref[...], preferred_element_type=F32)
    q = _head_norm_rope(q, qn_ref[...], gmat, cos, sin_a, sin_b)
    qb_ref[...] = (q * (MAP_W ** -0.5)).astype(BF16)


def _proj_kvq(h, ln_kv, ln_q, w_kv, w_q, k_norm, q_norm, gmat, cos, sin_a, sin_b):
    t = h.shape[0]
    tm = _row_tile(t, min(512, cos.shape[0]))
    n_tab = cos.shape[0] // tm
    row = pl.BlockSpec((tm, D_MODEL), lambda i: (i, 0))
    tab = pl.BlockSpec((tm, LANES), lambda i: (i % n_tab, 0))
    f32_out = jax.ShapeDtypeStruct((t, D_MODEL), F32)
    bf_out = jax.ShapeDtypeStruct((t, D_MODEL), BF16)
    return pl.pallas_call(
        _proj_kvq_kernel,
        grid=(t // tm,),
        in_specs=[row, _const_spec((1, D_MODEL)), _const_spec((1, D_MODEL)), _const_spec(w_kv.shape), _const_spec(w_q.shape),
                  _const_spec((1, LANES)), _const_spec((1, LANES)), _const_spec((LANES, LANES)), tab, tab, tab],
        out_specs=[row] * 5,
        out_shape=[f32_out, f32_out, bf_out, bf_out, bf_out],
        compiler_params=_params("parallel"),
        name="proj_kvq",
    )(h, ln_kv, ln_q, w_kv, w_q, k_norm, q_norm, gmat, cos, sin_a, sin_b)


def _norm_rope_t(x, gain, cos, sin):
    half = ROT_DIM // 2
    outs = []
    for g in range(HEAD_W // MAP_W):
        xg = x[g * MAP_W:(g + 1) * MAP_W, :]
        ms = jnp.mean(xg * xg, axis=0, keepdims=True)
        xn = xg * lax.rsqrt(ms + EPS) * gain[g * MAP_W:(g + 1) * MAP_W, :]
        x1, x2 = xn[0:half], xn[half:ROT_DIM]
        outs += [x1 * cos - x2 * sin, x2 * cos + x1 * sin, xn[ROT_DIM:]]
    return jnp.concatenate(outs, axis=0)


def _proj_kvq_t_kernel(h_ref, lnkv_ref, lnq_ref, wkvt_ref, wv_ref, wqt_ref, kn_ref, qn_ref, cos_ref, sin_ref,
                       kt_ref, v_ref, kb_ref, vt_ref, qt_ref):
    h = h_ref[...]
    tm = h.shape[0]
    hn = h * lax.rsqrt(jnp.mean(h * h, axis=-1, keepdims=True) + EPS)
    nkv = (hn * lnkv_ref[...]).astype(BF16)
    nq = (hn * lnq_ref[...]).astype(BF16)
    cos, sin = cos_ref[...], sin_ref[...]
    rep = tm // LANES
    kn = jnp.concatenate([kn_ref[...]] * rep, axis=1)
    qn = jnp.concatenate([qn_ref[...]] * rep, axis=1)
    kvt = lax.dot_general(wkvt_ref[...], nkv, _NT, preferred_element_type=F32)
    kt = jnp.concatenate([_norm_rope_t(kvt[hd * HEAD_W:(hd + 1) * HEAD_W, :], kn, cos, sin)
                          for hd in range(N_HEADS)], axis=0)
    kt_ref[0] = kt
    vt_ref[0] = kvt[D_MODEL:, :].astype(BF16)
    kb_ref[...] = kt.T.astype(BF16)
    v_ref[...] = kvt[D_MODEL:, :].T
    qt = lax.dot_general(wqt_ref[...], nq, _NT, preferred_element_type=F32)
    qt = jnp.concatenate([_norm_rope_t(qt[hd * HEAD_W:(hd + 1) * HEAD_W, :], qn, cos, sin)
                          for hd in range(N_HEADS)], axis=0)
    qt_ref[0] = (qt * (MAP_W ** -0.5 * LOG2E)).astype(BF16)


def _proj_kvq_t(h, bsz, ln_kv, ln_q, w_kvt, w_v, w_qt, kn_col, qn_col, cos_t, sin_t):
    t = h.shape[0]
    length = t // bsz
    tm = _row_tile(length, 512)
    nl = length // tm
    row = pl.BlockSpec((tm, D_MODEL), lambda b, i: (b * nl + i, 0))
    colb = pl.BlockSpec((1, D_MODEL, tm), lambda b, i: (b, 0, i))
    tab = pl.BlockSpec((ROT_DIM // 2, tm), lambda b, i: (0, i))
    t_f32 = jax.ShapeDtypeStruct((bsz, D_MODEL, length), F32)
    t_bf = jax.ShapeDtypeStruct((bsz, D_MODEL, length), BF16)
    return pl.pallas_call(
        _proj_kvq_t_kernel,
        grid=(bsz, nl),
        in_specs=[row, _const_spec((1, D_MODEL)), _const_spec((1, D_MODEL)), _const_spec(w_kvt.shape),
                  _const_spec(w_v.shape), _const_spec(w_qt.shape), _const_spec((HEAD_W, LANES)),
                  _const_spec((HEAD_W, LANES)), tab, tab],
        out_specs=[colb, row, row, colb, colb],
        out_shape=[t_f32, jax.ShapeDtypeStruct((t, D_MODEL), F32), jax.ShapeDtypeStruct((t, D_MODEL), BF16), t_bf, t_bf],
        compiler_params=_params("parallel", "parallel"),
        name="proj_kvq_t",
    )(h, ln_kv, ln_q, w_kvt, w_v, w_qt, kn_col, qn_col, cos_t, sin_t)


def _lam_from(lamv_ref, lam_init):
    lv = lamv_ref[...]
    s1 = jnp.sum(lv[0:1, :] * lv[1:2, :], axis=-1, keepdims=True)
    s2 = jnp.sum(lv[2:3, :] * lv[3:4, :], axis=-1, keepdims=True)
    return jnp.exp(s1) - jnp.exp(s2) + lam_init


def _attn_prompt_kernel(qt_ref, k_ref, vt_ref, lamv_ref, subln_ref, o_ref, m_ref, acc_ref,
                        s_ref, s1_ref, p_ref, p1_ref, *, bq, strip, lam_init):
    i = pl.program_id(2)
    qt = qt_ref[0]
    row = lax.broadcasted_iota(jnp.int32, (HEAD_W, bq), 0)
    zero = jnp.zeros_like(qt)
    q2t = jnp.concatenate([jnp.where(row < MAP_W, qt, zero), jnp.where(row >= MAP_W, qt, zero)], axis=1)
    strips = [slice(t * strip, (t + 1) * strip) for t in range(2 * bq // strip)]
    m_ref[...] = jnp.full(m_ref.shape, -jnp.inf, F32)
    acc_ref[...] = jnp.zeros_like(acc_ref)
    s_refs, p_refs = (s_ref, s1_ref), (p_ref, p1_ref)
    p1_ref[...] = jnp.zeros(p1_ref.shape, BF16)
    ones = jnp.ones((ONES_ROWS, bq), BF16)

    def keys(j):
        return k_ref[0, pl.ds(pl.multiple_of(j * bq, bq), bq), :]

    def values_t(j):
        vt = vt_ref[0, :, pl.ds(pl.multiple_of(j * bq, bq), bq)]
        return jnp.concatenate([vt, ones], axis=0)

    def scores(j, slot):
        kb = keys(j)
        for sl in strips:
            s_refs[slot][:, sl] = jnp.dot(kb, q2t[:, sl], preferred_element_type=F32)

    def softmax_update(j, cur, masked):
        vt_prev = values_t(jnp.maximum(j - 1, 0))
        for sl in strips:
            pv = jnp.dot(vt_prev, p_refs[1 - cur][:, sl], preferred_element_type=F32)
            s = s_refs[cur][:, sl]
            if masked:
                ki = lax.broadcasted_iota(jnp.int32, (bq, strip), 0)
                qi = (lax.broadcasted_iota(jnp.int32, (bq, strip), 1) + sl.start) % bq
                s = jnp.where(ki <= qi, s, -jnp.inf)
            m_old = m_ref[:, sl]
            m_new = jnp.maximum(m_old, jnp.max(s, axis=0, keepdims=True))
            alpha = jnp.exp2(m_old - m_new)
            pb = jnp.exp2(s - m_new).astype(BF16)
            acc = alpha * (acc_ref[:, sl] + pv)
            if masked:
                acc = acc + jnp.dot(values_t(j), pb, preferred_element_type=F32)
            else:
                p_refs[cur][:, sl] = pb
            acc_ref[:, sl] = acc
            m_ref[:, sl] = m_new

    def sub_step(j, cur):
        scores(j + 1, 1 - cur)
        softmax_update(j, cur, False)

    scores(0, 0)

    def body(t, carry):
        sub_step(2 * t, 0)
        sub_step(2 * t + 1, 1)
        return carry

    lax.fori_loop(0, i // 2, body, 0)
    odd = i % 2 == 1

    @pl.when(odd)
    def _():
        sub_step(i - 1, 0)
        softmax_update(i, 1, True)

    @pl.when(jnp.logical_not(odd))
    def _():
        softmax_update(i, 0, True)

    lam = _lam_from(lamv_ref, lam_init)
    o_all = acc_ref[0:HEAD_W, :] / acc_ref[HEAD_W:HEAD_W + 1, :]
    ot = o_all[:, :bq] - lam * o_all[:, bq:]
    gain = jnp.concatenate([subln_ref[...]] * (bq // LANES), axis=1)
    ot = ot * lax.rsqrt(jnp.mean(ot * ot, axis=0, keepdims=True) + EPS) * gain * (1.0 - lam_init)
    o_ref[0] = ot.T.astype(o_ref.dtype)


def _attn_prompt(qt, k, vt, lamv, subln_col, lam_init):
    bsz, length, _ = k.shape
    bq = min(512, length)
    assert length % bq == 0
    return pl.pallas_call(
        functools.partial(_attn_prompt_kernel, bq=bq, strip=min(512, bq), lam_init=lam_init),
        grid=(bsz, N_HEADS, length // bq),
        in_specs=[pl.BlockSpec((1, HEAD_W, bq), lambda b, h, i: (b, h, i)),
                  pl.BlockSpec((1, length, HEAD_W), lambda b, h, i: (b, 0, h)),
                  pl.BlockSpec((1, HEAD_W, length), lambda b, h, i: (b, h, 0)),
                  _const_spec(lamv.shape), _const_spec((HEAD_W, LANES))],
        out_specs=pl.BlockSpec((1, bq, HEAD_W), lambda b, h, i: (b, i, h)),
        out_shape=jax.ShapeDtypeStruct((bsz, length, D_MODEL), BF16),
        scratch_shapes=[pltpu.VMEM((1, 2 * bq), F32), pltpu.VMEM((HEAD_W + ONES_ROWS, 2 * bq), F32),
                        pltpu.VMEM((bq, 2 * bq), F32), pltpu.VMEM((bq, 2 * bq), F32),
                        pltpu.VMEM((bq, 2 * bq), BF16), pltpu.VMEM((bq, 2 * bq), BF16)],
        compiler_params=_params("parallel", "parallel", "arbitrary"),
        name="attn_prompt",
    )(qt, k, vt, lamv, subln_col)


def _attn_sample_kernel(pt_ref, q_ref, *refs, n_q, n_pages, lam_init):
    del pt_ref
    kc_refs = refs[:n_pages]
    vc_refs = refs[n_pages:2 * n_pages]
    kn_ref, vn_ref, lamv_ref, subln_ref, o_ref = refs[2 * n_pages:]
    grp = 2 * n_q
    q = q_ref[0].astype(F32)
    lane = lax.broadcasted_iota(jnp.int32, (n_q, HEAD_W), 1)
    r_q = lax.broadcasted_iota(jnp.int32, (grp, PAGE), 0) % n_q
    k_i = lax.broadcasted_iota(jnp.int32, (grp, PAGE), 1)
    new_mask = k_i <= r_q
    pad = jnp.zeros((PAGE - n_q, HEAD_W), BF16)
    lam = _lam_from(lamv_ref, lam_init)
    for h in range(N_HEADS):
        cols = slice(h * HEAD_W, (h + 1) * HEAD_W)
        qh = q[:, cols]
        qh2 = jnp.concatenate([jnp.where(lane < MAP_W, qh, 0.0), jnp.where(lane >= MAP_W, qh, 0.0)], axis=0).astype(BF16)
        s_pages = [jnp.dot(qh2, kc_refs[pg][0, h].astype(BF16), preferred_element_type=F32) for pg in range(n_pages)]
        k_new = jnp.concatenate([kn_ref[0, :, cols], pad], axis=0)
        s_new = jnp.where(new_mask, lax.dot_general(qh2, k_new, _NT, preferred_element_type=F32), -jnp.inf)
        s = jnp.concatenate(s_pages + [s_new], axis=1)
        m = jnp.max(s, axis=-1, keepdims=True)
        p = jnp.exp(s - m)
        l = jnp.sum(p, axis=-1, keepdims=True)
        pb = p.astype(BF16)
        v_new = jnp.concatenate([vn_ref[0, :, cols], pad], axis=0)
        acc = jnp.dot(pb[:, n_pages * PAGE:], v_new, preferred_element_type=F32)
        for pg in range(n_pages):
            v_h = vc_refs[pg][0, pl.ds(h, PAGE, stride=N_HEADS), :].astype(BF16)
            acc = acc + jnp.dot(pb[:, pg * PAGE:(pg + 1) * PAGE], v_h, preferred_element_type=F32)
        o_all = acc / l
        o = o_all[:n_q] - lam * o_all[n_q:]
        o_ref[0, :, cols] = (_rms(o, subln_ref[...]) * (1.0 - lam_init)).astype(o_ref.dtype)


def _attn_sample(q, cache_kt, cache_v2, page_table, k_new, v_new, lamv, subln, lam_init):
    bsz, n_q, _ = q.shape
    n_pages = page_table.shape[1]
    per_b = lambda b, pt: (b, 0, 0)
    k_specs = [pl.BlockSpec((1, N_HEADS, HEAD_W, PAGE), functools.partial(lambda b, pt, pg: (pt[b, pg], 0, 0, 0), pg=pg))
               for pg in range(n_pages)]
    v_specs = [pl.BlockSpec((1, PAGE * N_HEADS, HEAD_W), functools.partial(lambda b, pt, pg: (pt[b, pg], 0, 0), pg=pg))
               for pg in range(n_pages)]
    grid_spec = pltpu.PrefetchScalarGridSpec(
        num_scalar_prefetch=1,
        grid=(bsz,),
        in_specs=[pl.BlockSpec((1, n_q, D_MODEL), per_b)] + k_specs + v_specs
                 + [pl.BlockSpec((1, n_q, D_MODEL), per_b), pl.BlockSpec((1, n_q, D_MODEL), per_b),
                    pl.BlockSpec(lamv.shape, lambda b, pt: (0, 0)), pl.BlockSpec((1, HEAD_W), lambda b, pt: (0, 0))],
        out_specs=pl.BlockSpec((1, n_q, D_MODEL), per_b),
    )
    return pl.pallas_call(
        functools.partial(_attn_sample_kernel, n_q=n_q, n_pages=n_pages, lam_init=lam_init),
        grid_spec=grid_spec,
        out_shape=jax.ShapeDtypeStruct((bsz, n_q, D_MODEL), F32),
        compiler_params=_params("parallel"),
        name="attn_sample",
    )(page_table, q, *([cache_kt] * n_pages), *([cache_v2] * n_pages), k_new, v_new, lamv, subln)


def _rope_angles(pos):
    inv = ROPE_THETA ** (-jnp.arange(0, ROT_DIM, 2, dtype=F32) / ROT_DIM)
    ang = pos.astype(F32)[:, None] * inv[None, :]
    return jnp.cos(ang), jnp.sin(ang)


def _rope_tables(pos):
    half = ROT_DIM // 2
    cos, sin = _rope_angles(pos)
    n = pos.shape[0]
    ones = jnp.ones((n, MAP_W - ROT_DIM), F32)
    zeros = jnp.zeros((n, MAP_W - ROT_DIM), F32)
    z8 = jnp.zeros((n, half), F32)
    cos_g = jnp.concatenate([cos, cos, ones], axis=1)
    sa_g = jnp.concatenate([-sin, z8, zeros], axis=1)
    sb_g = jnp.concatenate([z8, sin, zeros], axis=1)
    rep = LANES // MAP_W
    return jnp.tile(cos_g, (1, rep)), jnp.tile(sa_g, (1, rep)), jnp.tile(sb_g, (1, rep))


def _pad_lanes(v):
    return jnp.zeros((1, LANES), F32).at[0, :v.shape[0]].set(v.astype(F32))


def _gain_col(g):
    return jnp.broadcast_to(jnp.tile(g.astype(F32), HEAD_W // MAP_W)[:, None], (HEAD_W, LANES))


def kernel(x_prompt, x_sample, cache_k, cache_v, page_table, state_delta, state_conv, ln_mix, ln_mlp, w_up, w_down,
           w_in_a, conv_a, a_log, dt_bias, o_norm_a, w_out_a, ln_kv, w_kv, k_norm, w_q_b, q_norm_b,
           lam_q1, lam_k1, lam_q2, lam_k2, subln_b, w_out_b):
    bsz, seq, _ = x_prompt.shape
    dbsz, dseq, _ = x_sample.shape
    n_pages = page_table.shape[1]
    past_len = n_pages * PAGE
    tp, ts = bsz * seq, dbsz * dseq
    row2 = lambda v: v.reshape(1, -1).astype(F32)

    w_in = w_in_a[0]
    w_main = w_in[:, :CONV_W + D_MODEL].astype(BF16)
    w_ba = jnp.zeros((D_MODEL, 2 * LANES), F32)
    w_ba = w_ba.at[:, :N_HEADS].set(w_in[:, CONV_W + D_MODEL:CONV_W + D_MODEL + N_HEADS])
    w_ba = w_ba.at[:, LANES:LANES + N_HEADS].set(w_in[:, CONV_W + D_MODEL + N_HEADS:]).astype(BF16)
    ln0 = row2(ln_mix[0])
    alog, dtb, onorm = _pad_lanes(a_log[0]), _pad_lanes(dt_bias[0]), row2(o_norm_a[0])

    xp2, xs2 = x_prompt.reshape(tp, D_MODEL), x_sample.reshape(ts, D_MODEL)
    qkv_p, z_p, ba_p = _proj_in(xp2, ln0, w_main, w_ba)
    qkv_s, z_s, ba_s = _proj_in(xs2, ln0, w_main, w_ba)

    nchunk_p = min(2, seq // DELTA_CHUNK)
    o_p, sd_p, ct_p = _delta(qkv_p.reshape(bsz, seq, CONV_W), z_p.reshape(bsz, seq, D_MODEL),
                             ba_p.reshape(bsz, seq, 2 * LANES), conv_a[0], alog, dtb, onorm,
                             chunk=DELTA_CHUNK, n_chunks=nchunk_p, group=1, bb=math.gcd(bsz, 2))
    chunk_s = math.gcd(dseq, DELTA_CHUNK)
    o_s, sd_s, ct_s = _delta(qkv_s.reshape(dbsz, dseq, CONV_W), z_s.reshape(dbsz, dseq, D_MODEL),
                             ba_s.reshape(dbsz, dseq, 2 * LANES), conv_a[0], alog, dtb, onorm,
                             chunk=chunk_s, n_chunks=dseq // chunk_s, bb=math.gcd(dbsz, 4),
                             s0=state_delta[0], conv_buf=state_conv[0], out_dtype=F32)

    wo_a, wup0, wdn0 = w_out_a[0].astype(BF16), w_up[0].astype(BF16), w_down[0].astype(BF16)
    h_p = _post(xp2, o_p.reshape(tp, D_MODEL), wo_a, row2(ln_mlp[0]), wup0, wdn0)
    h_s = _post(xs2, o_s.reshape(ts, D_MODEL), wo_a, row2(ln_mlp[0]), wup0, wdn0)

    lam_init = 0.8 - 0.6 * math.exp(-0.3 * 1)
    wkv, wq = w_kv.astype(BF16), w_q_b[0].astype(BF16)
    ln1 = row2(ln_mix[1])
    cos_p, sin_p = _rope_angles(jnp.arange(seq, dtype=jnp.int32))
    kt_p, v_p, kb_p, vt_p, qt_p = _proj_kvq_t(h_p, bsz, row2(ln_kv), ln1, wkv.T, wkv[:, D_MODEL:], wq.T,
                                              _gain_col(k_norm), _gain_col(q_norm_b[0]), cos_p.T, sin_p.T)
    kn = jnp.tile(k_norm.astype(F32), LANES // MAP_W).reshape(1, LANES)
    qn = jnp.tile(q_norm_b[0].astype(F32), LANES // MAP_W).reshape(1, LANES)
    gi = jnp.arange(LANES) // MAP_W
    gmat = (gi[:, None] == gi[None, :]).astype(F32) / MAP_W
    pos_s = past_len + jnp.arange(dseq, dtype=jnp.int32)
    tile_s = min(512, ts) // dseq
    cos_s, sa_s, sb_s = (jnp.tile(t, (tile_s, 1)) for t in _rope_tables(pos_s))
    k_s, v_s, kb_s, vb_s, qb_s = _proj_kvq(h_s, row2(ln_kv), ln1, wkv, wq, kn, qn, gmat, cos_s, sa_s, sb_s)

    lamv = jnp.zeros((SUBLANES, LANES), F32)
    for r, vec in enumerate((lam_q1[0], lam_k1[0], lam_q2[0], lam_k2[0])):
        lamv = lamv.at[r, :MAP_W].set(vec.astype(F32))
    subln = row2(subln_b[0])
    subln_col = jnp.broadcast_to(subln_b[0].astype(F32)[:, None], (HEAD_W, LANES))
    n_pool = cache_k.shape[0]
    cache_kt = jnp.transpose(cache_k, (0, 2, 3, 4, 1)).reshape(n_pool, N_HEADS, HEAD_W, PAGE)
    cache_v2 = cache_v.reshape(n_pool, PAGE * N_HEADS, HEAD_W)
    a_p = _attn_prompt(qt_p, kb_p.reshape(bsz, seq, D_MODEL), vt_p, lamv, subln_col, lam_init)
    a_s = _attn_sample(qb_s.reshape(dbsz, dseq, D_MODEL), cache_kt, cache_v2, page_table,
                       kb_s.reshape(dbsz, dseq, D_MODEL), vb_s.reshape(dbsz, dseq, D_MODEL), lamv, subln, lam_init)

    wo_b, wup1, wdn1 = w_out_b[0].astype(BF16), w_up[1].astype(BF16), w_down[1].astype(BF16)
    y_p = _post(h_p, a_p.reshape(tp, D_MODEL), wo_b, row2(ln_mlp[1]), wup1, wdn1)
    y_s = _post(h_s, a_s.reshape(ts, D_MODEL), wo_b, row2(ln_mlp[1]), wup1, wdn1)

    k_prompt = jnp.transpose(kt_p.reshape(bsz, N_HEADS, 2, MAP_W, seq), (0, 4, 1, 2, 3))
    return (y_p.reshape(bsz, seq, D_MODEL), y_s.reshape(dbsz, dseq, D_MODEL),
            k_prompt, v_p.reshape(bsz, seq, N_HEADS, HEAD_W),
            k_s.reshape(dbsz, dseq, N_HEADS, 2, MAP_W), v_s.reshape(dbsz, dseq, N_HEADS, HEAD_W),
            sd_p[None], sd_s[None], ct_p[None], ct_s[None])
```

```python
import functools
import math

import jax
import jax.numpy as jnp
from jax import lax
from jax.experimental import pallas as pl
from jax.experimental.pallas import tpu as pltpu

F32 = jnp.float32
BF16 = jnp.bfloat16
EPS = 1e-6

D_MODEL = 1024
D_FF = 4 * D_MODEL
N_HEADS = 8
HEAD_W = 128
MAP_W = 64
CONV_TAPS = 4
CONV_W = 3 * D_MODEL
DELTA_CHUNK = 64
ROT_DIM = 16
ROPE_THETA = 500000.0
PAGE = 128
LANES = 128
SUBLANES = 8
ONES_ROWS = 16
LOG2E = math.log2(math.e)
VMEM_LIMIT = 56 * 1024 * 1024

_NT = (((1,), (1,)), ((), ()))
_HI = lax.Precision.HIGHEST


def _dot(a, b):
    return jnp.dot(a.astype(BF16), b.astype(BF16), preferred_element_type=F32)


def _dot_nt(a, b):
    return lax.dot_general(a.astype(BF16), b.astype(BF16), _NT, preferred_element_type=F32)


def _rms(x, g):
    return x * lax.rsqrt(jnp.mean(x * x, axis=-1, keepdims=True) + EPS) * g


def _sigmoid(x):
    return 1.0 / (1.0 + jnp.exp(-x))


def _softplus(x):
    return jnp.maximum(x, 0.0) + jnp.log1p(jnp.exp(-jnp.abs(x)))


def _const_spec(shape):
    nd = len(shape)
    return pl.BlockSpec(shape, lambda *_: (0,) * nd, pipeline_mode=pl.Buffered(1))


def _params(*sem):
    return pltpu.CompilerParams(dimension_semantics=sem, vmem_limit_bytes=VMEM_LIMIT)


def _row_tile(t, want):
    tm = min(t, want)
    assert t % tm == 0
    return tm


def _proj_in_kernel(x_ref, ln_ref, w_ref, wba_ref, qkv_ref, z_ref, ba_ref):
    h = _rms(x_ref[...], ln_ref[...]).astype(BF16)
    qkv_ref[...] = jnp.dot(h, w_ref[:, :CONV_W], preferred_element_type=F32)
    z_ref[...] = jnp.dot(h, w_ref[:, CONV_W:], preferred_element_type=F32)
    ba_ref[...] = jnp.dot(h, wba_ref[...], preferred_element_type=F32)


def _proj_in(x, ln, w_main, w_ba):
    t = x.shape[0]
    tm = _row_tile(t, 512)
    row = lambda w: pl.BlockSpec((tm, w), lambda i: (i, 0))
    return pl.pallas_call(
        _proj_in_kernel,
        grid=(t // tm,),
        in_specs=[row(D_MODEL), _const_spec((1, D_MODEL)), _const_spec(w_main.shape), _const_spec(w_ba.shape)],
        out_specs=[row(CONV_W), row(D_MODEL), row(2 * LANES)],
        out_shape=[jax.ShapeDtypeStruct((t, CONV_W), F32), jax.ShapeDtypeStruct((t, D_MODEL), F32),
                   jax.ShapeDtypeStruct((t, 2 * LANES), F32)],
        compiler_params=_params("parallel"),
        name="proj_in",
    )(x, ln, w_main, w_ba)


def _delta_kernel(*refs, chunk, n_chunks, bb, has_init):
    if has_init:
        (qkv_ref, z_ref, ba_ref, s0_ref, cb_ref, cw_ref, alog_ref, dtb_ref, onorm_ref,
         o_ref, sfin_ref, ctail_ref,
         xp_ref, q_s, k_s, v_s, g_s, beta_s, s_ref, wq_s, ub_s, at_s, kd_s, ee_s) = refs
    else:
        (qkv_ref, z_ref, ba_ref, cw_ref, alog_ref, dtb_ref, onorm_ref,
         o_ref, sfin_ref, ctail_ref,
         xp_ref, q_s, k_s, v_s, g_s, beta_s, s_ref, wq_s, ub_s, at_s, kd_s, ee_s) = refs
    c = chunk
    blk = c * n_chunks
    i = pl.program_id(1)
    last = pl.num_programs(1) - 1
    halo = SUBLANES
    hist = CONV_TAPS - 1

    @pl.when(i == 0)
    def _():
        xp_ref[:, 0:halo, :] = jnp.zeros((bb, halo, CONV_W), F32)
        if has_init:
            s_ref[...] = s0_ref[...]
            xp_ref[:, halo - hist:halo, :] = cb_ref[...]
        else:
            s_ref[...] = jnp.zeros_like(s_ref)

    for bi in range(bb):
        xp_ref[bi, halo:halo + blk, :] = qkv_ref[bi]
        ba = ba_ref[bi]
        beta_s[bi] = _sigmoid(ba[:, :LANES])
        g_s[bi] = -jnp.exp(alog_ref[...]) * _softplus(ba[:, LANES:] + dtb_ref[...])

    def conv_chunk(ci):
        r0 = ci * c
        for bi in range(bb):
            for j in range(CONV_W // LANES):
                cols = slice(j * LANES, (j + 1) * LANES)
                xg = xp_ref[bi, r0:r0 + halo + c, cols]
                acc = xg * cw_ref[0:1, cols]
                for w in range(1, CONV_TAPS):
                    acc = pltpu.roll(acc, 1, axis=0) + xg * cw_ref[w:w + 1, cols]
                acc = acc[halo:, :]
                y = acc / (1.0 + jnp.exp2(acc * (-LOG2E)))
                if j < 2 * N_HEADS:
                    inv_norm = lax.rsqrt(jnp.sum(y * y, axis=-1, keepdims=True) + EPS)
                if j < N_HEADS:
                    q_s[bi, r0:r0 + c, cols] = y * (inv_norm * (HEAD_W ** -0.5))
                elif j < 2 * N_HEADS:
                    k_s[bi, r0:r0 + c, (j - N_HEADS) * LANES:(j - N_HEADS + 1) * LANES] = y * inv_norm
                else:
                    v_s[bi, r0:r0 + c, (j - 2 * N_HEADS) * LANES:(j - 2 * N_HEADS + 1) * LANES] = y

    row = lax.broadcasted_iota(jnp.int32, (c, c), 0)
    col = lax.broadcasted_iota(jnp.int32, (c, c), 1)
    incl = row >= col
    strict = row > col
    ltri = incl.astype(F32)
    eye_c = (row == col).astype(F32)
    onorm = onorm_ref[...]
    n_sq = int(math.log2(c)) - 1
    heads = range(N_HEADS)
    hcols = [slice(h * LANES, (h + 1) * LANES) for h in heads]

    def phase_a(groups):
        inst = [(gi, h) for gi in range(len(groups)) for h in heads]
        rows, cg, cg_t, beta_c, ecg, ekd, bec = [], [], [], [], [], [], []
        for bi, ci in groups:
            r0 = ci * c if isinstance(ci, int) else pl.multiple_of(ci * c, c)
            rs = pl.ds(r0, c)
            g_c = g_s[bi, rs, :]
            b_c = beta_s[bi, rs, :]
            cg_c = jnp.dot(ltri, g_c, preferred_element_type=F32, precision=_HI)
            cg_t.append(cg_c.T)
            cg_last = cg_c[c - 1:c, :]
            e_c = jnp.exp(cg_c)
            ee_s[bi, ci] = jnp.exp(cg_last)
            rows.append(rs)
            cg.append(cg_c)
            beta_c.append(b_c)
            ecg.append(e_c)
            ekd.append(jnp.exp(cg_last - cg_c))
            bec.append(b_c * e_c)
        q = [q_s[groups[gi][0], rows[gi], hcols[h]] for gi, h in inst]
        k = [k_s[groups[gi][0], rows[gi], hcols[h]] for gi, h in inst]
        v = [v_s[groups[gi][0], rows[gi], hcols[h]] for gi, h in inst]
        kb = [x.astype(BF16) for x in k]
        kk = [_dot_nt(x, x) for x in kb]
        qk = [_dot_nt(a, b) for a, b in zip(q, kb)]
        beta_h = [beta_c[gi][:, h:h + 1] for gi, h in inst]
        diff = [cg[gi][:, h:h + 1] - cg_t[gi][h:h + 1, :] for gi, h in inst]
        decay = [jnp.where(incl, jnp.exp(jnp.where(incl, d, 0.0)), 0.0) for d in diff]
        nmat = [-(jnp.where(strict, dc * x, 0.0) * b) for dc, x, b in zip(decay, kk, beta_h)]
        pmat = [eye_c + n for n in nmat]
        qmat = [_dot(n, n) for n in nmat]
        for it in range(n_sq):
            if it < n_sq - 1:
                pq = [_dot(jnp.concatenate([p, n], axis=0), n) for p, n in zip(pmat, qmat)]
                pmat = [p + x[:c] for p, x in zip(pmat, pq)]
                qmat = [x[c:] for x in pq]
            else:
                pmat = [p + _dot(p, n) for p, n in zip(pmat, qmat)]
        rhs = [jnp.concatenate([kx * bec[gi][:, h:h + 1], vx * b], axis=1)
               for (gi, h), kx, vx, b in zip(inst, k, v, beta_h)]
        sol = [_dot(p, r) for p, r in zip(pmat, rhs)]
        k_dec = [kx * ekd[gi][:, h:h + 1] for (gi, h), kx in zip(inst, k)]
        kd_t = [x.T for x in k_dec]
        for n, (gi, h) in enumerate(inst):
            bi, ci = groups[gi]
            q_dec = q[n] * ecg[gi][:, h:h + 1]
            wq_s[bi, ci, h] = jnp.concatenate([sol[n][:, :LANES], q_dec], axis=0).astype(BF16)
            ub_s[bi, ci, h] = sol[n][:, LANES:]
            at_s[bi, ci, h] = decay[n] * qk[n]
            kd_s[bi, ci, h] = kd_t[n]

    def phase_b(ci):
        inst = [(bi, h) for bi in range(bb) for h in heads]
        r0 = ci * c if isinstance(ci, int) else pl.multiple_of(ci * c, c)
        rs = pl.ds(r0, c)
        s_old = [s_ref[bi, h] for bi, h in inst]
        r = [jnp.dot(wq_s[bi, ci, h], s.astype(BF16), preferred_element_type=F32) for (bi, h), s in zip(inst, s_old)]
        u = [ub_s[bi, ci, h] - x[:c] for (bi, h), x in zip(inst, r)]
        ub16 = [x.astype(BF16) for x in u]
        o = [x[c:] + _dot(at_s[bi, ci, h], y) for (bi, h), x, y in zip(inst, r, ub16)]
        for (bi, h), s, y in zip(inst, s_old, ub16):
            s_ref[bi, h] = s * ee_s[bi, ci][:, h:h + 1] + _dot(kd_s[bi, ci, h], y)
        for (bi, h), x in zip(inst, o):
            zt = z_ref[bi, rs, hcols[h]]
            o_ref[bi, rs, hcols[h]] = (_rms(x, onorm) * (zt * _sigmoid(zt))).astype(o_ref.dtype)

    conv_chunk(0)
    for ci in range(n_chunks):
        if ci + 1 < n_chunks:
            conv_chunk(ci + 1)
        phase_a([(bi, ci) for bi in range(bb)])

    for bi in range(bb):
        xp_ref[bi, 0:halo, :] = xp_ref[bi, blk:blk + halo, :]

        @pl.when(i == last)
        def _():
            ctail_ref[bi] = xp_ref[bi, halo - hist:halo, :]

    if n_chunks == 1:
        phase_b(0)
    else:
        def b_body(ci, carry):
            phase_b(ci)
            return carry

        lax.fori_loop(0, n_chunks, b_body, 0)

    @pl.when(i == last)
    def _():
        sfin_ref[...] = s_ref[...]


def _delta(qkv, z, ba, conv_w, a_log, dt_bias, o_norm, *, chunk, n_chunks, bb=1,
           s0=None, conv_buf=None, out_dtype=BF16):
    bsz, length, _ = qkv.shape
    blk = chunk * n_chunks
    assert length % blk == 0 and blk >= SUBLANES and bsz % bb == 0
    has_init = s0 is not None
    seq = lambda w: pl.BlockSpec((bb, blk, w), lambda b, i: (b, i, 0))
    state = pl.BlockSpec((bb, N_HEADS, HEAD_W, HEAD_W), lambda b, i: (b, 0, 0, 0))
    ctail = pl.BlockSpec((bb, CONV_TAPS - 1, CONV_W), lambda b, i: (b, 0, 0))
    in_specs = [seq(CONV_W), seq(D_MODEL), seq(2 * LANES)]
    args = [qkv, z, ba]
    if has_init:
        in_specs += [state, ctail]
        args += [s0, conv_buf]
    in_specs += [_const_spec(conv_w.shape), _const_spec((1, LANES)), _const_spec((1, LANES)), _const_spec((1, HEAD_W))]
    args += [conv_w, a_log, dt_bias, o_norm]
    per = (bb, n_chunks, N_HEADS)
    return pl.pallas_call(
        functools.partial(_delta_kernel, chunk=chunk, n_chunks=n_chunks, bb=bb, has_init=has_init),
        grid=(bsz // bb, length // blk),
        in_specs=in_specs,
        out_specs=[seq(D_MODEL), state, ctail],
        out_shape=[jax.ShapeDtypeStruct((bsz, length, D_MODEL), out_dtype),
                   jax.ShapeDtypeStruct((bsz, N_HEADS, HEAD_W, HEAD_W), F32),
                   jax.ShapeDtypeStruct((bsz, CONV_TAPS - 1, CONV_W), F32)],
        scratch_shapes=[pltpu.VMEM((bb, blk + SUBLANES, CONV_W), F32),
                        pltpu.VMEM((bb, blk, D_MODEL), F32), pltpu.VMEM((bb, blk, D_MODEL), F32),
                        pltpu.VMEM((bb, blk, D_MODEL), F32),
                        pltpu.VMEM((bb, blk, LANES), F32), pltpu.VMEM((bb, blk, LANES), F32),
                        pltpu.VMEM((bb, N_HEADS, HEAD_W, HEAD_W), F32),
                        pltpu.VMEM(per + (2 * chunk, HEAD_W), BF16), pltpu.VMEM(per + (chunk, HEAD_W), F32),
                        pltpu.VMEM(per + (chunk, chunk), F32), pltpu.VMEM(per + (HEAD_W, chunk), F32),
                        pltpu.VMEM((bb, n_chunks, 1, LANES), F32)],
        compiler_params=_params("parallel", "arbitrary"),
        name="delta_rule",
    )(*args)


def _post_kernel(x_ref, o_ref, wo_ref, ln_ref, wup_ref, wdn_ref, y_ref, *, ff_chunk):
    h1 = x_ref[...] + jnp.dot(o_ref[...].astype(BF16), wo_ref[...], preferred_element_type=F32)
    n = _rms(h1, ln_ref[...]).astype(BF16)
    acc = h1
    for c0 in range(0, D_FF, ff_chunk):
        u = jnp.dot(n, wup_ref[:, c0:c0 + ff_chunk], preferred_element_type=F32)
        a = jnp.square(jnp.maximum(u, 0.0)).astype(BF16)
        acc = acc + jnp.dot(a, wdn_ref[c0:c0 + ff_chunk, :], preferred_element_type=F32)
    y_ref[...] = acc


def _post(x, o, w_out, ln, w_up, w_down):
    t = x.shape[0]
    tm = _row_tile(t, 512)
    row = pl.BlockSpec((tm, D_MODEL), lambda i: (i, 0))
    return pl.pallas_call(
        functools.partial(_post_kernel, ff_chunk=1024),
        grid=(t // tm,),
        in_specs=[row, row, _const_spec(w_out.shape), _const_spec((1, D_MODEL)),
                  _const_spec(w_up.shape), _const_spec(w_down.shape)],
        out_specs=row,
        out_shape=jax.ShapeDtypeStruct((t, D_MODEL), F32),
        compiler_params=_params("parallel"),
        name="post_mlp",
    )(x, o, w_out, ln, w_up, w_down)


def _head_norm_rope(x, gain, gmat, cos, sin_a, sin_b):
    outs = []
    for j in range(D_MODEL // LANES):
        xt = x[:, j * LANES:(j + 1) * LANES]
        ms = jnp.dot(xt * xt, gmat, preferred_element_type=F32, precision=_HI)
        xn = xt * lax.rsqrt(ms + EPS) * gain
        outs.append(xn * cos + pltpu.roll(xn, LANES - ROT_DIM // 2, axis=1) * sin_a
                    + pltpu.roll(xn, ROT_DIM // 2, axis=1) * sin_b)
    return jnp.concatenate(outs, axis=1)


def _proj_kvq_kernel(h_ref, lnkv_ref, lnq_ref, wkv_ref, wq_ref, kn_ref, qn_ref, gmat_ref, cos_ref, sa_ref, sb_ref,
                     k_ref, v_ref, kb_ref, vb_ref, qb_ref):
    h = h_ref[...]
    cos, sin_a, sin_b, gmat = cos_ref[...], sa_ref[...], sb_ref[...], gmat_ref[...]
    nkv = _rms(h, lnkv_ref[...]).astype(BF16)
    k = jnp.dot(nkv, wkv_ref[:, :D_MODEL], preferred_element_type=F32)
    v = jnp.dot(nkv, wkv_ref[:, D_MODEL:], preferred_element_type=F32)
    k = _head_norm_rope(k, kn_ref[...], gmat, cos, sin_a, sin_b)
    k_ref[...] = k
    v_ref[...] = v
    kb_ref[...] = k.astype(BF16)
    vb_ref[...] = v.astype(BF16)
    nq = _rms(h, lnq_ref[...]).astype(BF16)
    q = jnp.dot(nq, wq_ref[...], preferred_element_type=F32)
    q = _head_norm_rope(q, qn_ref[...], gmat, cos, sin_a, sin_b)
    qb_ref[...] = (q * (MAP_W ** -0.5)).astype(BF16)


def _proj_kvq(h, ln_kv, ln_q, w_kv, w_q, k_norm, q_norm, gmat, cos, sin_a, sin_b):
    t = h.shape[0]
    tm = _row_tile(t, min(512, cos.shape[0]))
    n_tab = cos.shape[0] // tm
    row = pl.BlockSpec((tm, D_MODEL), lambda i: (i, 0))
    tab = pl.BlockSpec((tm, LANES), lambda i: (i % n_tab, 0))
    f32_out = jax.ShapeDtypeStruct((t, D_MODEL), F32)
    bf_out = jax.ShapeDtypeStruct((t, D_MODEL), BF16)
    return pl.pallas_call(
        _proj_kvq_kernel,
        grid=(t // tm,),
        in_specs=[row, _const_spec((1, D_MODEL)), _const_spec((1, D_MODEL)), _const_spec(w_kv.shape), _const_spec(w_q.shape),
                  _const_spec((1, LANES)), _const_spec((1, LANES)), _const_spec((LANES, LANES)), tab, tab, tab],
        out_specs=[row] * 5,
        out_shape=[f32_out, f32_out, bf_out, bf_out, bf_out],
        compiler_params=_params("parallel"),
        name="proj_kvq",
    )(h, ln_kv, ln_q, w_kv, w_q, k_norm, q_norm, gmat, cos, sin_a, sin_b)


def _norm_rope_t(x, gain, cos, sin):
    half = ROT_DIM // 2
    outs = []
    for g in range(HEAD_W // MAP_W):
        xg = x[g * MAP_W:(g + 1) * MAP_W, :]
        ms = jnp.mean(xg * xg, axis=0, keepdims=True)
        xn = xg * lax.rsqrt(ms + EPS) * gain[g * MAP_W:(g + 1) * MAP_W, :]
        x1, x2 = xn[0:half], xn[half:ROT_DIM]
        outs += [x1 * cos - x2 * sin, x2 * cos + x1 * sin, xn[ROT_DIM:]]
    return jnp.concatenate(outs, axis=0)


def _proj_kvq_t_kernel(h_ref, lnkv_ref, lnq_ref, wkvt_ref, wv_ref, wqt_ref, kn_ref, qn_ref, cos_ref, sin_ref,
                       kt_ref, v_ref, kb_ref, vt_ref, qt_ref):
    h = h_ref[...]
    tm = h.shape[0]
    hn = h * lax.rsqrt(jnp.mean(h * h, axis=-1, keepdims=True) + EPS)
    nkv = (hn * lnkv_ref[...]).astype(BF16)
    nq = (hn * lnq_ref[...]).astype(BF16)
    cos, sin = cos_ref[...], sin_ref[...]
    rep = tm // LANES
    kn = jnp.concatenate([kn_ref[...]] * rep, axis=1)
    qn = jnp.concatenate([qn_ref[...]] * rep, axis=1)
    kvt = lax.dot_general(wkvt_ref[...], nkv, _NT, preferred_element_type=F32)
    kt = jnp.concatenate([_norm_rope_t(kvt[hd * HEAD_W:(hd + 1) * HEAD_W, :], kn, cos, sin)
                          for hd in range(N_HEADS)], axis=0)
    kt_ref[0] = kt
    vt_ref[0] = kvt[D_MODEL:, :].astype(BF16)
    kb_ref[...] = kt.T.astype(BF16)
    v_ref[...] = kvt[D_MODEL:, :].T
    qt = lax.dot_general(wqt_ref[...], nq, _NT, preferred_element_type=F32)
    qt = jnp.concatenate([_norm_rope_t(qt[hd * HEAD_W:(hd + 1) * HEAD_W, :], qn, cos, sin)
                          for hd in range(N_HEADS)], axis=0)
    qt_ref[0] = (qt * (MAP_W ** -0.5 * LOG2E)).astype(BF16)


def _proj_kvq_t(h, bsz, ln_kv, ln_q, w_kvt, w_v, w_qt, kn_col, qn_col, cos_t, sin_t):
    t = h.shape[0]
    length = t // bsz
    tm = _row_tile(length, 512)
    nl = length // tm
    row = pl.BlockSpec((tm, D_MODEL), lambda b, i: (b * nl + i, 0))
    colb = pl.BlockSpec((1, D_MODEL, tm), lambda b, i: (b, 0, i))
    tab = pl.BlockSpec((ROT_DIM // 2, tm), lambda b, i: (0, i))
    t_f32 = jax.ShapeDtypeStruct((bsz, D_MODEL, length), F32)
    t_bf = jax.ShapeDtypeStruct((bsz, D_MODEL, length), BF16)
    return pl.pallas_call(
        _proj_kvq_t_kernel,
        grid=(bsz, nl),
        in_specs=[row, _const_spec((1, D_MODEL)), _const_spec((1, D_MODEL)), _const_spec(w_kvt.shape),
                  _const_spec(w_v.shape), _const_spec(w_qt.shape), _const_spec((HEAD_W, LANES)),
                  _const_spec((HEAD_W, LANES)), tab, tab],
        out_specs=[colb, row, row, colb, colb],
        out_shape=[t_f32, jax.ShapeDtypeStruct((t, D_MODEL), F32), jax.ShapeDtypeStruct((t, D_MODEL), BF16), t_bf, t_bf],
        compiler_params=_params("parallel", "parallel"),
        name="proj_kvq_t",
    )(h, ln_kv, ln_q, w_kvt, w_v, w_qt, kn_col, qn_col, cos_t, sin_t)


def _lam_from(lamv_ref, lam_init):
    lv = lamv_ref[...]
    s1 = jnp.sum(lv[0:1, :] * lv[1:2, :], axis=-1, keepdims=True)
    s2 = jnp.sum(lv[2:3, :] * lv[3:4, :], axis=-1, keepdims=True)
    return jnp.exp(s1) - jnp.exp(s2) + lam_init


def _attn_prompt_kernel(qt_ref, k_ref, vt_ref, lamv_ref, subln_ref, o_ref, m_ref, acc_ref,
                        s_ref, s1_ref, p_ref, p1_ref, *, bq, strip, lam_init):
    i = pl.program_id(2)
    qt = qt_ref[0]
    row = lax.broadcasted_iota(jnp.int32, (HEAD_W, bq), 0)
    zero = jnp.zeros_like(qt)
    q2t = jnp.concatenate([jnp.where(row < MAP_W, qt, zero), jnp.where(row >= MAP_W, qt, zero)], axis=1)
    strips = [slice(t * strip, (t + 1) * strip) for t in range(2 * bq // strip)]
    m_ref[...] = jnp.full(m_ref.shape, -jnp.inf, F32)
    acc_ref[...] = jnp.zeros_like(acc_ref)
    s_refs, p_refs = (s_ref, s1_ref), (p_ref, p1_ref)
    p1_ref[...] = jnp.zeros(p1_ref.shape, BF16)
    ones = jnp.ones((ONES_ROWS, bq), BF16)

    def keys(j):
        return k_ref[0, pl.ds(pl.multiple_of(j * bq, bq), bq), :]

    def values_t(j):
        vt = vt_ref[0, :, pl.ds(pl.multiple_of(j * bq, bq), bq)]
        return jnp.concatenate([vt, ones], axis=0)

    def scores(j, slot):
        kb = keys(j)
        for sl in strips:
            s_refs[slot][:, sl] = jnp.dot(kb, q2t[:, sl], preferred_element_type=F32)

    def softmax_update(j, cur):
        vt_prev = values_t(jnp.maximum(j - 1, 0))
        for sl in strips:
            pv = jnp.dot(vt_prev, p_refs[1 - cur][:, sl], preferred_element_type=F32)
            s = s_refs[cur][:, sl]
            m_old = m_ref[:, sl]
            m_new = jnp.maximum(m_old, jnp.max(s, axis=0, keepdims=True))
            alpha = jnp.exp2(m_old - m_new)
            p_refs[cur][:, sl] = jnp.exp2(s - m_new).astype(BF16)
            acc_ref[:, sl] = alpha * (acc_ref[:, sl] + pv)
            m_ref[:, sl] = m_new

    def diagonal_update(cur):
        half = bq // 2
        vt_prev = values_t(jnp.maximum(i - 1, 0))
        vt_own = values_t(i)
        tri = (lax.broadcasted_iota(jnp.int32, (half, half), 0) <= lax.broadcasted_iota(jnp.int32, (half, half), 1))
        for mp in range(2):
            for qh in range(2):
                sl = slice(mp * bq + qh * half, mp * bq + (qh + 1) * half)
                pv = jnp.dot(vt_prev, p_refs[1 - cur][:, sl], preferred_element_type=F32)
                m_old = m_ref[:, sl]
                parts = [jnp.where(tri, s_refs[cur][qh * half:(qh + 1) * half, sl], -jnp.inf)]
                if qh == 1:
                    parts.insert(0, s_refs[cur][0:half, sl])
                m_new = m_old
                for s in parts:
                    m_new = jnp.maximum(m_new, jnp.max(s, axis=0, keepdims=True))
                acc = jnp.exp2(m_old - m_new) * (acc_ref[:, sl] + pv)
                for kh, s in enumerate(parts):
                    acc = acc + jnp.dot(vt_own[:, kh * half:(kh + 1) * half], jnp.exp2(s - m_new).astype(BF16),
                                        preferred_element_type=F32)
                acc_ref[:, sl] = acc
                m_ref[:, sl] = m_new

    def sub_step(j, cur):
        scores(j + 1, 1 - cur)
        softmax_update(j, cur)

    scores(0, 0)

    def body(t, carry):
        sub_step(2 * t, 0)
        sub_step(2 * t + 1, 1)
        return carry

    lax.fori_loop(0, i // 2, body, 0)
    odd = i % 2 == 1

    @pl.when(odd)
    def _():
        sub_step(i - 1, 0)
        diagonal_update(1)

    @pl.when(jnp.logical_not(odd))
    def _():
        diagonal_update(0)

    lam = _lam_from(lamv_ref, lam_init)
    o_all = acc_ref[0:HEAD_W, :] / acc_ref[HEAD_W:HEAD_W + 1, :]
    ot = o_all[:, :bq] - lam * o_all[:, bq:]
    gain = jnp.concatenate([subln_ref[...]] * (bq // LANES), axis=1)
    ot = ot * lax.rsqrt(jnp.mean(ot * ot, axis=0, keepdims=True) + EPS) * gain * (1.0 - lam_init)
    o_ref[0] = ot.T.astype(o_ref.dtype)


def _attn_prompt(qt, k, vt, lamv, subln_col, lam_init):
    bsz, length, _ = k.shape
    bq = min(512, length)
    assert length % bq == 0
    return pl.pallas_call(
        functools.partial(_attn_prompt_kernel, bq=bq, strip=min(512, bq), lam_init=lam_init),
        grid=(bsz, N_HEADS, length // bq),
        in_specs=[pl.BlockSpec((1, HEAD_W, bq), lambda b, h, i: (b, h, i)),
                  pl.BlockSpec((1, length, HEAD_W), lambda b, h, i: (b, 0, h)),
                  pl.BlockSpec((1, HEAD_W, length), lambda b, h, i: (b, h, 0)),
                  _const_spec(lamv.shape), _const_spec((HEAD_W, LANES))],
        out_specs=pl.BlockSpec((1, bq, HEAD_W), lambda b, h, i: (b, i, h)),
        out_shape=jax.ShapeDtypeStruct((bsz, length, D_MODEL), BF16),
        scratch_shapes=[pltpu.VMEM((1, 2 * bq), F32), pltpu.VMEM((HEAD_W + ONES_ROWS, 2 * bq), F32),
                        pltpu.VMEM((bq, 2 * bq), F32), pltpu.VMEM((bq, 2 * bq), F32),
                        pltpu.VMEM((bq, 2 * bq), BF16), pltpu.VMEM((bq, 2 * bq), BF16)],
        compiler_params=_params("parallel", "parallel", "arbitrary"),
        name="attn_prompt",
    )(qt, k, vt, lamv, subln_col)


def _attn_sample_kernel(pt_ref, q_ref, *refs, n_q, n_pages, lam_init):
    del pt_ref
    kc_refs = refs[:n_pages]
    vc_refs = refs[n_pages:2 * n_pages]
    kn_ref, vn_ref, lamv_ref, subln_ref, o_ref = refs[2 * n_pages:]
    grp = 2 * n_q
    q = q_ref[0].astype(F32)
    lane = lax.broadcasted_iota(jnp.int32, (n_q, HEAD_W), 1)
    r_q = lax.broadcasted_iota(jnp.int32, (grp, PAGE), 0) % n_q
    k_i = lax.broadcasted_iota(jnp.int32, (grp, PAGE), 1)
    new_mask = k_i <= r_q
    pad = jnp.zeros((PAGE - n_q, HEAD_W), BF16)
    lam = _lam_from(lamv_ref, lam_init)
    for h in range(N_HEADS):
        cols = slice(h * HEAD_W, (h + 1) * HEAD_W)
        qh = q[:, cols]
        qh2 = jnp.concatenate([jnp.where(lane < MAP_W, qh, 0.0), jnp.where(lane >= MAP_W, qh, 0.0)], axis=0).astype(BF16)
        s_pages = [jnp.dot(qh2, kc_refs[pg][0, h].astype(BF16), preferred_element_type=F32) for pg in range(n_pages)]
        k_new = jnp.concatenate([kn_ref[0, :, cols], pad], axis=0)
        s_new = jnp.where(new_mask, lax.dot_general(qh2, k_new, _NT, preferred_element_type=F32), -jnp.inf)
        s = jnp.concatenate(s_pages + [s_new], axis=1)
        m = jnp.max(s, axis=-1, keepdims=True)
        p = jnp.exp(s - m)
        l = jnp.sum(p, axis=-1, keepdims=True)
        pb = p.astype(BF16)
        v_new = jnp.concatenate([vn_ref[0, :, cols], pad], axis=0)
        acc = jnp.dot(pb[:, n_pages * PAGE:], v_new, preferred_element_type=F32)
        for pg in range(n_pages):
            v_h = vc_refs[pg][0, pl.ds(h, PAGE, stride=N_HEADS), :].astype(BF16)
            acc = acc + jnp.dot(pb[:, pg * PAGE:(pg + 1) * PAGE], v_h, preferred_element_type=F32)
        o_all = acc / l
        o = o_all[:n_q] - lam * o_all[n_q:]
        o_ref[0, :, cols] = (_rms(o, subln_ref[...]) * (1.0 - lam_init)).astype(o_ref.dtype)


def _attn_sample(q, cache_kt, cache_v2, page_table, k_new, v_new, lamv, subln, lam_init):
    bsz, n_q, _ = q.shape
    n_pages = page_table.shape[1]
    per_b = lambda b, pt: (b, 0, 0)
    k_specs = [pl.BlockSpec((1, N_HEADS, HEAD_W, PAGE), functools.partial(lambda b, pt, pg: (pt[b, pg], 0, 0, 0), pg=pg))
               for pg in range(n_pages)]
    v_specs = [pl.BlockSpec((1, PAGE * N_HEADS, HEAD_W), functools.partial(lambda b, pt, pg: (pt[b, pg], 0, 0), pg=pg))
               for pg in range(n_pages)]
    grid_spec = pltpu.PrefetchScalarGridSpec(
        num_scalar_prefetch=1,
        grid=(bsz,),
        in_specs=[pl.BlockSpec((1, n_q, D_MODEL), per_b)] + k_specs + v_specs
                 + [pl.BlockSpec((1, n_q, D_MODEL), per_b), pl.BlockSpec((1, n_q, D_MODEL), per_b),
                    pl.BlockSpec(lamv.shape, lambda b, pt: (0, 0)), pl.BlockSpec((1, HEAD_W), lambda b, pt: (0, 0))],
        out_specs=pl.BlockSpec((1, n_q, D_MODEL), per_b),
    )
    return pl.pallas_call(
        functools.partial(_attn_sample_kernel, n_q=n_q, n_pages=n_pages, lam_init=lam_init),
        grid_spec=grid_spec,
        out_shape=jax.ShapeDtypeStruct((bsz, n_q, D_MODEL), F32),
        compiler_params=_params("parallel"),
        name="attn_sample",
    )(page_table, q, *([cache_kt] * n_pages), *([cache_v2] * n_pages), k_new, v_new, lamv, subln)


def _rope_angles(pos):
    inv = ROPE_THETA ** (-jnp.arange(0, ROT_DIM, 2, dtype=F32) / ROT_DIM)
    ang = pos.astype(F32)[:, None] * inv[None, :]
    return jnp.cos(ang), jnp.sin(ang)


def _rope_tables(pos):
    half = ROT_DIM // 2
    cos, sin = _rope_angles(pos)
    n = pos.shape[0]
    ones = jnp.ones((n, MAP_W - ROT_DIM), F32)
    zeros = jnp.zeros((n, MAP_W - ROT_DIM), F32)
    z8 = jnp.zeros((n, half), F32)
    cos_g = jnp.concatenate([cos, cos, ones], axis=1)
    sa_g = jnp.concatenate([-sin, z8, zeros], axis=1)
    sb_g = jnp.concatenate([z8, sin, zeros], axis=1)
    rep = LANES // MAP_W
    return jnp.tile(cos_g, (1, rep)), jnp.tile(sa_g, (1, rep)), jnp.tile(sb_g, (1, rep))


def _pad_lanes(v):
    return jnp.zeros((1, LANES), F32).at[0, :v.shape[0]].set(v.astype(F32))


def _gain_col(g):
    return jnp.broadcast_to(jnp.tile(g.astype(F32), HEAD_W // MAP_W)[:, None], (HEAD_W, LANES))


def kernel(x_prompt, x_sample, cache_k, cache_v, page_table, state_delta, state_conv, ln_mix, ln_mlp, w_up, w_down,
           w_in_a, conv_a, a_log, dt_bias, o_norm_a, w_out_a, ln_kv, w_kv, k_norm, w_q_b, q_norm_b,
           lam_q1, lam_k1, lam_q2, lam_k2, subln_b, w_out_b):
    bsz, seq, _ = x_prompt.shape
    dbsz, dseq, _ = x_sample.shape
    n_pages = page_table.shape[1]
    past_len = n_pages * PAGE
    tp, ts = bsz * seq, dbsz * dseq
    row2 = lambda v: v.reshape(1, -1).astype(F32)

    w_in = w_in_a[0]
    w_main = w_in[:, :CONV_W + D_MODEL].astype(BF16)
    w_ba = jnp.zeros((D_MODEL, 2 * LANES), F32)
    w_ba = w_ba.at[:, :N_HEADS].set(w_in[:, CONV_W + D_MODEL:CONV_W + D_MODEL + N_HEADS])
    w_ba = w_ba.at[:, LANES:LANES + N_HEADS].set(w_in[:, CONV_W + D_MODEL + N_HEADS:]).astype(BF16)
    ln0 = row2(ln_mix[0])
    alog, dtb, onorm = _pad_lanes(a_log[0]), _pad_lanes(dt_bias[0]), row2(o_norm_a[0])

    xp2, xs2 = x_prompt.reshape(tp, D_MODEL), x_sample.reshape(ts, D_MODEL)
    qkv_p, z_p, ba_p = _proj_in(xp2, ln0, w_main, w_ba)
    qkv_s, z_s, ba_s = _proj_in(xs2, ln0, w_main, w_ba)

    nchunk_p = min(2, seq // DELTA_CHUNK)
    o_p, sd_p, ct_p = _delta(qkv_p.reshape(bsz, seq, CONV_W), z_p.reshape(bsz, seq, D_MODEL),
                             ba_p.reshape(bsz, seq, 2 * LANES), conv_a[0], alog, dtb, onorm,
                             chunk=DELTA_CHUNK, n_chunks=nchunk_p, bb=math.gcd(bsz, 2))
    chunk_s = math.gcd(dseq, DELTA_CHUNK)
    o_s, sd_s, ct_s = _delta(qkv_s.reshape(dbsz, dseq, CONV_W), z_s.reshape(dbsz, dseq, D_MODEL),
                             ba_s.reshape(dbsz, dseq, 2 * LANES), conv_a[0], alog, dtb, onorm,
                             chunk=chunk_s, n_chunks=dseq // chunk_s, bb=math.gcd(dbsz, 4),
                             s0=state_delta[0], conv_buf=state_conv[0], out_dtype=F32)

    wo_a, wup0, wdn0 = w_out_a[0].astype(BF16), w_up[0].astype(BF16), w_down[0].astype(BF16)
    h_p = _post(xp2, o_p.reshape(tp, D_MODEL), wo_a, row2(ln_mlp[0]), wup0, wdn0)
    h_s = _post(xs2, o_s.reshape(ts, D_MODEL), wo_a, row2(ln_mlp[0]), wup0, wdn0)

    lam_init = 0.8 - 0.6 * math.exp(-0.3 * 1)
    wkv, wq = w_kv.astype(BF16), w_q_b[0].astype(BF16)
    ln1 = row2(ln_mix[1])
    cos_p, sin_p = _rope_angles(jnp.arange(seq, dtype=jnp.int32))
    kt_p, v_p, kb_p, vt_p, qt_p = _proj_kvq_t(h_p, bsz, row2(ln_kv), ln1, wkv.T, wkv[:, D_MODEL:], wq.T,
                                              _gain_col(k_norm), _gain_col(q_norm_b[0]), cos_p.T, sin_p.T)
    kn = jnp.tile(k_norm.astype(F32), LANES // MAP_W).reshape(1, LANES)
    qn = jnp.tile(q_norm_b[0].astype(F32), LANES // MAP_W).reshape(1, LANES)
    gi = jnp.arange(LANES) // MAP_W
    gmat = (gi[:, None] == gi[None, :]).astype(F32) / MAP_W
    pos_s = past_len + jnp.arange(dseq, dtype=jnp.int32)
    tile_s = min(512, ts) // dseq
    cos_s, sa_s, sb_s = (jnp.tile(t, (tile_s, 1)) for t in _rope_tables(pos_s))
    k_s, v_s, kb_s, vb_s, qb_s = _proj_kvq(h_s, row2(ln_kv), ln1, wkv, wq, kn, qn, gmat, cos_s, sa_s, sb_s)

    lamv = jnp.zeros((SUBLANES, LANES), F32)
    for r, vec in enumerate((lam_q1[0], lam_k1[0], lam_q2[0], lam_k2[0])):
        lamv = lamv.at[r, :MAP_W].set(vec.astype(F32))
    subln = row2(subln_b[0])
    subln_col = jnp.broadcast_to(subln_b[0].astype(F32)[:, None], (HEAD_W, LANES))
    n_pool = cache_k.shape[0]
    cache_kt = jnp.transpose(cache_k, (0, 2, 3, 4, 1)).reshape(n_pool, N_HEADS, HEAD_W, PAGE)
    cache_v2 = cache_v.reshape(n_pool, PAGE * N_HEADS, HEAD_W)
    a_p = _attn_prompt(qt_p, kb_p.reshape(bsz, seq, D_MODEL), vt_p, lamv, subln_col, lam_init)
    a_s = _attn_sample(qb_s.reshape(dbsz, dseq, D_MODEL), cache_kt, cache_v2, page_table,
                       kb_s.reshape(dbsz, dseq, D_MODEL), vb_s.reshape(dbsz, dseq, D_MODEL), lamv, subln, lam_init)

    wo_b, wup1, wdn1 = w_out_b[0].astype(BF16), w_up[1].astype(BF16), w_down[1].astype(BF16)
    y_p = _post(h_p, a_p.reshape(tp, D_MODEL), wo_b, row2(ln_mlp[1]), wup1, wdn1)
    y_s = _post(h_s, a_s.reshape(ts, D_MODEL), wo_b, row2(ln_mlp[1]), wup1, wdn1)

    k_prompt = jnp.transpose(kt_p.reshape(bsz, N_HEADS, 2, MAP_W, seq), (0, 4, 1, 2, 3))
    return (y_p.reshape(bsz, seq, D_MODEL), y_s.reshape(dbsz, dseq, D_MODEL),
            k_prompt, v_p.reshape(bsz, seq, N_HEADS, HEAD_W),
            k_s.reshape(dbsz, dseq, N_HEADS, 2, MAP_W), v_s.reshape(dbsz, dseq, N_HEADS, HEAD_W),
            sd_p[None], sd_s[None], ct_p[None], ct_s[None])
```

```python
import functools
import math

import jax
import jax.numpy as jnp
from jax import lax
from jax.experimental import pallas as pl
from jax.experimental.pallas import tpu as pltpu

F32 = jnp.float32
BF16 = jnp.bfloat16
EPS = 1e-6

D_MODEL = 1024
D_FF = 4 * D_MODEL
N_HEADS = 8
HEAD_W = 128
MAP_W = 64
CONV_TAPS = 4
CONV_W = 3 * D_MODEL
DELTA_CHUNK = 64
ROT_DIM = 16
ROPE_THETA = 500000.0
PAGE = 128
LANES = 128
SUBLANES = 8
ONES_ROWS = 16
LOG2E = math.log2(math.e)
VMEM_LIMIT = 56 * 1024 * 1024

ROW_TILE = 512
FF_CHUNK = 1024
ATTN_BLOCK = 512
ATTN_STRIP = 512
DELTA_STEP_CHUNKS = 2
DELTA_STEP_SEQS = 2
DELTA_STEP_SEQS_SAMPLED = 8

_NT = (((1,), (1,)), ((), ()))
_HI = lax.Precision.HIGHEST


def _dot(a, b):
    return jnp.dot(a.astype(BF16), b.astype(BF16), preferred_element_type=F32)


def _dot_nt(a, b):
    return lax.dot_general(a.astype(BF16), b.astype(BF16), _NT, preferred_element_type=F32)


def _rms(x, g):
    return x * lax.rsqrt(jnp.mean(x * x, axis=-1, keepdims=True) + EPS) * g


def _sigmoid(x):
    return 1.0 / (1.0 + jnp.exp(-x))


def _softplus(x):
    return jnp.maximum(x, 0.0) + jnp.log1p(jnp.exp(-jnp.abs(x)))


def _const_spec(shape):
    nd = len(shape)
    return pl.BlockSpec(shape, lambda *_: (0,) * nd, pipeline_mode=pl.Buffered(1))


def _params(*sem):
    return pltpu.CompilerParams(dimension_semantics=sem, vmem_limit_bytes=VMEM_LIMIT)


def _row_tile(t, want):
    tm = min(t, want)
    assert t % tm == 0
    return tm


def _proj_in_kernel(x_ref, ln_ref, w_ref, wba_ref, qkv_ref, z_ref, ba_ref):
    h = _rms(x_ref[...], ln_ref[...]).astype(BF16)
    qkv_ref[...] = jnp.dot(h, w_ref[:, :CONV_W], preferred_element_type=F32)
    z_ref[...] = jnp.dot(h, w_ref[:, CONV_W:], preferred_element_type=F32)
    ba_ref[...] = jnp.dot(h, wba_ref[...], preferred_element_type=F32)


def _proj_in(x, ln, w_main, w_ba):
    t = x.shape[0]
    tm = _row_tile(t, ROW_TILE)
    row = lambda w: pl.BlockSpec((tm, w), lambda i: (i, 0))
    return pl.pallas_call(
        _proj_in_kernel,
        grid=(t // tm,),
        in_specs=[row(D_MODEL), _const_spec((1, D_MODEL)), _const_spec(w_main.shape), _const_spec(w_ba.shape)],
        out_specs=[row(CONV_W), row(D_MODEL), row(2 * LANES)],
        out_shape=[jax.ShapeDtypeStruct((t, CONV_W), F32), jax.ShapeDtypeStruct((t, D_MODEL), F32),
                   jax.ShapeDtypeStruct((t, 2 * LANES), F32)],
        compiler_params=_params("parallel"),
        name="proj_in",
    )(x, ln, w_main, w_ba)


def _delta_kernel(*refs, chunk, n_chunks, bb, has_init):
    if has_init:
        (qkv_ref, z_ref, ba_ref, s0_ref, cb_ref, cw_ref, alog_ref, dtb_ref, onorm_ref,
         o_ref, sfin_ref, ctail_ref,
         xp_ref, q_s, k_s, v_s, g_s, beta_s, s_ref, wq_s, ub_s, at_s, kd_s, ee_s) = refs
    else:
        (qkv_ref, z_ref, ba_ref, cw_ref, alog_ref, dtb_ref, onorm_ref,
         o_ref, sfin_ref, ctail_ref,
         xp_ref, q_s, k_s, v_s, g_s, beta_s, s_ref, wq_s, ub_s, at_s, kd_s, ee_s) = refs
    c = chunk
    blk = c * n_chunks
    i = pl.program_id(1)
    last = pl.num_programs(1) - 1
    halo = SUBLANES
    hist = CONV_TAPS - 1

    @pl.when(i == 0)
    def _():
        xp_ref[:, 0:halo, :] = jnp.zeros((bb, halo, CONV_W), F32)
        if has_init:
            s_ref[...] = s0_ref[...]
            xp_ref[:, halo - hist:halo, :] = cb_ref[...]
        else:
            s_ref[...] = jnp.zeros_like(s_ref)

    for bi in range(bb):
        xp_ref[bi, halo:halo + blk, :] = qkv_ref[bi]
        ba = ba_ref[bi]
        beta_s[bi] = _sigmoid(ba[:, :LANES])
        g_s[bi] = -jnp.exp(alog_ref[...]) * _softplus(ba[:, LANES:] + dtb_ref[...])

    def conv_chunk(ci):
        r0 = ci * c
        for bi in range(bb):
            for j in range(CONV_W // LANES):
                cols = slice(j * LANES, (j + 1) * LANES)
                xg = xp_ref[bi, r0:r0 + halo + c, cols]
                acc = xg * cw_ref[0:1, cols]
                for w in range(1, CONV_TAPS):
                    acc = pltpu.roll(acc, 1, axis=0) + xg * cw_ref[w:w + 1, cols]
                acc = acc[halo:, :]
                y = acc / (1.0 + jnp.exp2(acc * (-LOG2E)))
                if j < 2 * N_HEADS:
                    inv_norm = lax.rsqrt(jnp.sum(y * y, axis=-1, keepdims=True) + EPS)
                if j < N_HEADS:
                    q_s[bi, r0:r0 + c, cols] = y * (inv_norm * (HEAD_W ** -0.5))
                elif j < 2 * N_HEADS:
                    k_s[bi, r0:r0 + c, (j - N_HEADS) * LANES:(j - N_HEADS + 1) * LANES] = y * inv_norm
                else:
                    v_s[bi, r0:r0 + c, (j - 2 * N_HEADS) * LANES:(j - 2 * N_HEADS + 1) * LANES] = y

    row = lax.broadcasted_iota(jnp.int32, (c, c), 0)
    col = lax.broadcasted_iota(jnp.int32, (c, c), 1)
    incl = row >= col
    strict = row > col
    ltri = incl.astype(F32)
    eye_c = (row == col).astype(F32)
    onorm = onorm_ref[...]
    n_sq = int(math.log2(c)) - 1
    heads = range(N_HEADS)
    hcols = [slice(h * LANES, (h + 1) * LANES) for h in heads]

    def phase_a(groups):
        inst = [(gi, h) for gi in range(len(groups)) for h in heads]
        rows, cg, cg_t, beta_c, ecg, ekd, bec = [], [], [], [], [], [], []
        for bi, ci in groups:
            r0 = ci * c if isinstance(ci, int) else pl.multiple_of(ci * c, c)
            rs = pl.ds(r0, c)
            g_c = g_s[bi, rs, :]
            b_c = beta_s[bi, rs, :]
            cg_c = jnp.dot(ltri, g_c, preferred_element_type=F32, precision=_HI)
            cg_t.append(cg_c.T)
            cg_last = cg_c[c - 1:c, :]
            e_c = jnp.exp(cg_c)
            ee_s[bi, ci] = jnp.exp(cg_last)
            rows.append(rs)
            cg.append(cg_c)
            beta_c.append(b_c)
            ecg.append(e_c)
            ekd.append(jnp.exp(cg_last - cg_c))
            bec.append(b_c * e_c)
        q = [q_s[groups[gi][0], rows[gi], hcols[h]] for gi, h in inst]
        k = [k_s[groups[gi][0], rows[gi], hcols[h]] for gi, h in inst]
        v = [v_s[groups[gi][0], rows[gi], hcols[h]] for gi, h in inst]
        kb = [x.astype(BF16) for x in k]
        kk = [_dot_nt(x, x) for x in kb]
        qk = [_dot_nt(a, b) for a, b in zip(q, kb)]
        beta_h = [beta_c[gi][:, h:h + 1] for gi, h in inst]
        diff = [cg[gi][:, h:h + 1] - cg_t[gi][h:h + 1, :] for gi, h in inst]
        decay = [jnp.where(incl, jnp.exp(jnp.where(incl, d, 0.0)), 0.0) for d in diff]
        nmat = [-(jnp.where(strict, dc * x, 0.0) * b) for dc, x, b in zip(decay, kk, beta_h)]
        pmat = [eye_c + n for n in nmat]
        qmat = [_dot(n, n) for n in nmat]
        for it in range(n_sq):
            if it < n_sq - 1:
                pq = [_dot(jnp.concatenate([p, n], axis=0), n) for p, n in zip(pmat, qmat)]
                pmat = [p + x[:c] for p, x in zip(pmat, pq)]
                qmat = [x[c:] for x in pq]
            else:
                pmat = [p + _dot(p, n) for p, n in zip(pmat, qmat)]
        rhs = [jnp.concatenate([kx * bec[gi][:, h:h + 1], vx * b], axis=1)
               for (gi, h), kx, vx, b in zip(inst, k, v, beta_h)]
        sol = [_dot(p, r) for p, r in zip(pmat, rhs)]
        k_dec = [kx * ekd[gi][:, h:h + 1] for (gi, h), kx in zip(inst, k)]
        kd_t = [x.T for x in k_dec]
        for n, (gi, h) in enumerate(inst):
            bi, ci = groups[gi]
            q_dec = q[n] * ecg[gi][:, h:h + 1]
            wq_s[bi, ci, h] = jnp.concatenate([sol[n][:, :LANES], q_dec], axis=0).astype(BF16)
            ub_s[bi, ci, h] = sol[n][:, LANES:]
            at_s[bi, ci, h] = decay[n] * qk[n]
            kd_s[bi, ci, h] = kd_t[n]

    def phase_b(ci):
        inst = [(bi, h) for bi in range(bb) for h in heads]
        r0 = ci * c if isinstance(ci, int) else pl.multiple_of(ci * c, c)
        rs = pl.ds(r0, c)
        s_old = [s_ref[bi, h] for bi, h in inst]
        r = [jnp.dot(wq_s[bi, ci, h], s.astype(BF16), preferred_element_type=F32) for (bi, h), s in zip(inst, s_old)]
        u = [ub_s[bi, ci, h] - x[:c] for (bi, h), x in zip(inst, r)]
        ub16 = [x.astype(BF16) for x in u]
        o = [x[c:] + _dot(at_s[bi, ci, h], y) for (bi, h), x, y in zip(inst, r, ub16)]
        for (bi, h), s, y in zip(inst, s_old, ub16):
            s_ref[bi, h] = s * ee_s[bi, ci][:, h:h + 1] + _dot(kd_s[bi, ci, h], y)
        for (bi, h), x in zip(inst, o):
            zt = z_ref[bi, rs, hcols[h]]
            o_ref[bi, rs, hcols[h]] = (_rms(x, onorm) * (zt * _sigmoid(zt))).astype(o_ref.dtype)

    conv_chunk(0)
    for ci in range(n_chunks):
        if ci + 1 < n_chunks:
            conv_chunk(ci + 1)
        phase_a([(bi, ci) for bi in range(bb)])

    for bi in range(bb):
        xp_ref[bi, 0:halo, :] = xp_ref[bi, blk:blk + halo, :]

        @pl.when(i == last)
        def _():
            ctail_ref[bi] = xp_ref[bi, halo - hist:halo, :]

    if n_chunks == 1:
        phase_b(0)
    else:
        def b_body(ci, carry):
            phase_b(ci)
            return carry

        lax.fori_loop(0, n_chunks, b_body, 0)

    @pl.when(i == last)
    def _():
        sfin_ref[...] = s_ref[...]


def _delta(qkv, z, ba, conv_w, a_log, dt_bias, o_norm, *, chunk, n_chunks, bb=1,
           s0=None, conv_buf=None, out_dtype=BF16):
    bsz, length, _ = qkv.shape
    blk = chunk * n_chunks
    assert length % blk == 0 and blk >= SUBLANES and bsz % bb == 0
    has_init = s0 is not None
    seq = lambda w: pl.BlockSpec((bb, blk, w), lambda b, i: (b, i, 0))
    state = pl.BlockSpec((bb, N_HEADS, HEAD_W, HEAD_W), lambda b, i: (b, 0, 0, 0))
    ctail = pl.BlockSpec((bb, CONV_TAPS - 1, CONV_W), lambda b, i: (b, 0, 0))
    in_specs = [seq(CONV_W), seq(D_MODEL), seq(2 * LANES)]
    args = [qkv, z, ba]
    if has_init:
        in_specs += [state, ctail]
        args += [s0, conv_buf]
    in_specs += [_const_spec(conv_w.shape), _const_spec((1, LANES)), _const_spec((1, LANES)), _const_spec((1, HEAD_W))]
    args += [conv_w, a_log, dt_bias, o_norm]
    per = (bb, n_chunks, N_HEADS)
    return pl.pallas_call(
        functools.partial(_delta_kernel, chunk=chunk, n_chunks=n_chunks, bb=bb, has_init=has_init),
        grid=(bsz // bb, length // blk),
        in_specs=in_specs,
        out_specs=[seq(D_MODEL), state, ctail],
        out_shape=[jax.ShapeDtypeStruct((bsz, length, D_MODEL), out_dtype),
                   jax.ShapeDtypeStruct((bsz, N_HEADS, HEAD_W, HEAD_W), F32),
                   jax.ShapeDtypeStruct((bsz, CONV_TAPS - 1, CONV_W), F32)],
        scratch_shapes=[pltpu.VMEM((bb, blk + SUBLANES, CONV_W), F32),
                        pltpu.VMEM((bb, blk, D_MODEL), F32), pltpu.VMEM((bb, blk, D_MODEL), F32),
                        pltpu.VMEM((bb, blk, D_MODEL), F32),
                        pltpu.VMEM((bb, blk, LANES), F32), pltpu.VMEM((bb, blk, LANES), F32),
                        pltpu.VMEM((bb, N_HEADS, HEAD_W, HEAD_W), F32),
                        pltpu.VMEM(per + (2 * chunk, HEAD_W), BF16), pltpu.VMEM(per + (chunk, HEAD_W), F32),
                        pltpu.VMEM(per + (chunk, chunk), F32), pltpu.VMEM(per + (HEAD_W, chunk), F32),
                        pltpu.VMEM((bb, n_chunks, 1, LANES), F32)],
        compiler_params=_params("parallel", "arbitrary"),
        name="delta_rule",
    )(*args)


def _post_kernel(x_ref, o_ref, wo_ref, ln_ref, wup_ref, wdn_ref, y_ref, *, ff_chunk):
    h1 = x_ref[...] + jnp.dot(o_ref[...].astype(BF16), wo_ref[...], preferred_element_type=F32)
    n = _rms(h1, ln_ref[...]).astype(BF16)
    acc = h1
    for c0 in range(0, D_FF, ff_chunk):
        u = jnp.dot(n, wup_ref[:, c0:c0 + ff_chunk], preferred_element_type=F32)
        a = jnp.square(jnp.maximum(u, 0.0)).astype(BF16)
        acc = acc + jnp.dot(a, wdn_ref[c0:c0 + ff_chunk, :], preferred_element_type=F32)
    y_ref[...] = acc


def _post(x, o, w_out, ln, w_up, w_down):
    t = x.shape[0]
    tm = _row_tile(t, ROW_TILE)
    row = pl.BlockSpec((tm, D_MODEL), lambda i: (i, 0))
    return pl.pallas_call(
        functools.partial(_post_kernel, ff_chunk=FF_CHUNK),
        grid=(t // tm,),
        in_specs=[row, row, _const_spec(w_out.shape), _const_spec((1, D_MODEL)),
                  _const_spec(w_up.shape), _const_spec(w_down.shape)],
        out_specs=row,
        out_shape=jax.ShapeDtypeStruct((t, D_MODEL), F32),
        compiler_params=_params("parallel"),
        name="post_mlp",
    )(x, o, w_out, ln, w_up, w_down)


def _head_norm_rope(x, gain, gmat, cos, sin_a, sin_b):
    outs = []
    for j in range(D_MODEL // LANES):
        xt = x[:, j * LANES:(j + 1) * LANES]
        ms = jnp.dot(xt * xt, gmat, preferred_element_type=F32, precision=_HI)
        xn = xt * lax.rsqrt(ms + EPS) * gain
        outs.append(xn * cos + pltpu.roll(xn, LANES - ROT_DIM // 2, axis=1) * sin_a
                    + pltpu.roll(xn, ROT_DIM // 2, axis=1) * sin_b)
    return jnp.concatenate(outs, axis=1)


def _proj_kvq_kernel(h_ref, lnkv_ref, lnq_ref, wkv_ref, wq_ref, kn_ref, qn_ref, gmat_ref, cos_ref, sa_ref, sb_ref,
                     k_ref, v_ref, kb_ref, vb_ref, qb_ref):
    h = h_ref[...]
    cos, sin_a, sin_b, gmat = cos_ref[...], sa_ref[...], sb_ref[...], gmat_ref[...]
    nkv = _rms(h, lnkv_ref[...]).astype(BF16)
    k = jnp.dot(nkv, wkv_ref[:, :D_MODEL], preferred_element_type=F32)
    v = jnp.dot(nkv, wkv_ref[:, D_MODEL:], preferred_element_type=F32)
    k = _head_norm_rope(k, kn_ref[...], gmat, cos, sin_a, sin_b)
    k_ref[...] = k
    v_ref[...] = v
    kb_ref[...] = k.astype(BF16)
    vb_ref[...] = v.astype(BF16)
    nq = _rms(h, lnq_ref[...]).astype(BF16)
    q = jnp.dot(nq, wq_ref[...], preferred_element_type=F32)
    q = _head_norm_rope(q, qn_ref[...], gmat, cos, sin_a, sin_b)
    qb_ref[...] = (q * (MAP_W ** -0.5)).astype(BF16)


def _proj_kvq(h, ln_kv, ln_q, w_kv, w_q, k_norm, q_norm, gmat, cos, sin_a, sin_b):
    t = h.shape[0]
    tm = _row_tile(t, min(ROW_TILE, cos.shape[0]))
    n_tab = cos.shape[0] // tm
    row = pl.BlockSpec((tm, D_MODEL), lambda i: (i, 0))
    tab = pl.BlockSpec((tm, LANES), lambda i: (i % n_tab, 0))
    f32_out = jax.ShapeDtypeStruct((t, D_MODEL), F32)
    bf_out = jax.ShapeDtypeStruct((t, D_MODEL), BF16)
    return pl.pallas_call(
        _proj_kvq_kernel,
        grid=(t // tm,),
        in_specs=[row, _const_spec((1, D_MODEL)), _const_spec((1, D_MODEL)), _const_spec(w_kv.shape), _const_spec(w_q.shape),
                  _const_spec((1, LANES)), _const_spec((1, LANES)), _const_spec((LANES, LANES)), tab, tab, tab],
        out_specs=[row] * 5,
        out_shape=[f32_out, f32_out, bf_out, bf_out, bf_out],
        compiler_params=_params("parallel"),
        name="proj_kvq",
    )(h, ln_kv, ln_q, w_kv, w_q, k_norm, q_norm, gmat, cos, sin_a, sin_b)


def _norm_rope_t(x, gain, cos, sin):
    half = ROT_DIM // 2
    outs = []
    for g in range(HEAD_W // MAP_W):
        xg = x[g * MAP_W:(g + 1) * MAP_W, :]
        ms = jnp.mean(xg * xg, axis=0, keepdims=True)
        xn = xg * lax.rsqrt(ms + EPS) * gain[g * MAP_W:(g + 1) * MAP_W, :]
        x1, x2 = xn[0:half], xn[half:ROT_DIM]
        outs += [x1 * cos - x2 * sin, x2 * cos + x1 * sin, xn[ROT_DIM:]]
    return jnp.concatenate(outs, axis=0)


def _proj_kvq_t_kernel(h_ref, lnkv_ref, lnq_ref, wkvt_ref, wv_ref, wqt_ref, kn_ref, qn_ref, cos_ref, sin_ref,
                       kt_ref, v_ref, kb_ref, vt_ref, qt_ref):
    h = h_ref[...]
    tm = h.shape[0]
    hn = h * lax.rsqrt(jnp.mean(h * h, axis=-1, keepdims=True) + EPS)
    nkv = (hn * lnkv_ref[...]).astype(BF16)
    nq = (hn * lnq_ref[...]).astype(BF16)
    cos, sin = cos_ref[...], sin_ref[...]
    rep = tm // LANES
    kn = jnp.concatenate([kn_ref[...]] * rep, axis=1)
    qn = jnp.concatenate([qn_ref[...]] * rep, axis=1)
    kvt = lax.dot_general(wkvt_ref[...], nkv, _NT, preferred_element_type=F32)
    kt = jnp.concatenate([_norm_rope_t(kvt[hd * HEAD_W:(hd + 1) * HEAD_W, :], kn, cos, sin)
                          for hd in range(N_HEADS)], axis=0)
    kt_ref[0] = kt
    vt_ref[0] = kvt[D_MODEL:, :].astype(BF16)
    kb_ref[...] = kt.T.astype(BF16)
    v_ref[...] = kvt[D_MODEL:, :].T
    qt = lax.dot_general(wqt_ref[...], nq, _NT, preferred_element_type=F32)
    qt = jnp.concatenate([_norm_rope_t(qt[hd * HEAD_W:(hd + 1) * HEAD_W, :], qn, cos, sin)
                          for hd in range(N_HEADS)], axis=0)
    qt_ref[0] = (qt * (MAP_W ** -0.5 * LOG2E)).astype(BF16)


def _proj_kvq_t(h, bsz, ln_kv, ln_q, w_kvt, w_v, w_qt, kn_col, qn_col, cos_t, sin_t):
    t = h.shape[0]
    length = t // bsz
    tm = _row_tile(length, ROW_TILE)
    nl = length // tm
    row = pl.BlockSpec((tm, D_MODEL), lambda b, i: (b * nl + i, 0))
    colb = pl.BlockSpec((1, D_MODEL, tm), lambda b, i: (b, 0, i))
    tab = pl.BlockSpec((ROT_DIM // 2, tm), lambda b, i: (0, i))
    t_f32 = jax.ShapeDtypeStruct((bsz, D_MODEL, length), F32)
    t_bf = jax.ShapeDtypeStruct((bsz, D_MODEL, length), BF16)
    return pl.pallas_call(
        _proj_kvq_t_kernel,
        grid=(bsz, nl),
        in_specs=[row, _const_spec((1, D_MODEL)), _const_spec((1, D_MODEL)), _const_spec(w_kvt.shape),
                  _const_spec(w_v.shape), _const_spec(w_qt.shape), _const_spec((HEAD_W, LANES)),
                  _const_spec((HEAD_W, LANES)), tab, tab],
        out_specs=[colb, row, row, colb, colb],
        out_shape=[t_f32, jax.ShapeDtypeStruct((t, D_MODEL), F32), jax.ShapeDtypeStruct((t, D_MODEL), BF16), t_bf, t_bf],
        compiler_params=_params("parallel", "parallel"),
        name="proj_kvq_t",
    )(h, ln_kv, ln_q, w_kvt, w_v, w_qt, kn_col, qn_col, cos_t, sin_t)


def _lam_from(lamv_ref, lam_init):
    lv = lamv_ref[...]
    s1 = jnp.sum(lv[0:1, :] * lv[1:2, :], axis=-1, keepdims=True)
    s2 = jnp.sum(lv[2:3, :] * lv[3:4, :], axis=-1, keepdims=True)
    return jnp.exp(s1) - jnp.exp(s2) + lam_init


def _attn_prompt_kernel(qt_ref, k_ref, vt_ref, lamv_ref, subln_ref, o_ref, m_ref, acc_ref,
                        s_ref, s1_ref, p_ref, p1_ref, *, bq, strip, lam_init):
    i = pl.program_id(2)
    qt = qt_ref[0]
    row = lax.broadcasted_iota(jnp.int32, (HEAD_W, bq), 0)
    zero = jnp.zeros_like(qt)
    q2t = jnp.concatenate([jnp.where(row < MAP_W, qt, zero), jnp.where(row >= MAP_W, qt, zero)], axis=1)
    strips = [slice(t * strip, (t + 1) * strip) for t in range(2 * bq // strip)]
    m_ref[...] = jnp.full(m_ref.shape, -jnp.inf, F32)
    acc_ref[...] = jnp.zeros_like(acc_ref)
    s_refs, p_refs = (s_ref, s1_ref), (p_ref, p1_ref)
    p1_ref[...] = jnp.zeros(p1_ref.shape, BF16)
    ones = jnp.ones((ONES_ROWS, bq), BF16)

    def keys(j):
        return k_ref[0, pl.ds(pl.multiple_of(j * bq, bq), bq), :]

    def values_t(j):
        vt = vt_ref[0, :, pl.ds(pl.multiple_of(j * bq, bq), bq)]
        return jnp.concatenate([vt, ones], axis=0)

    def scores(j, slot):
        kb = keys(j)
        for sl in strips:
            s_refs[slot][:, sl] = jnp.dot(kb, q2t[:, sl], preferred_element_type=F32)

    def softmax_update(j, cur):
        vt_prev = values_t(jnp.maximum(j - 1, 0))
        for sl in strips:
            pv = jnp.dot(vt_prev, p_refs[1 - cur][:, sl], preferred_element_type=F32)
            s = s_refs[cur][:, sl]
            m_old = m_ref[:, sl]
            m_new = jnp.maximum(m_old, jnp.max(s, axis=0, keepdims=True))
            alpha = jnp.exp2(m_old - m_new)
            p_refs[cur][:, sl] = jnp.exp2(s - m_new).astype(BF16)
            acc_ref[:, sl] = alpha * (acc_ref[:, sl] + pv)
            m_ref[:, sl] = m_new

    def diagonal_update(cur):
        half = bq // 2
        vt_prev = values_t(jnp.maximum(i - 1, 0))
        vt_own = values_t(i)
        tri = (lax.broadcasted_iota(jnp.int32, (half, half), 0) <= lax.broadcasted_iota(jnp.int32, (half, half), 1))
        for mp in range(2):
            for qh in range(2):
                sl = slice(mp * bq + qh * half, mp * bq + (qh + 1) * half)
                pv = jnp.dot(vt_prev, p_refs[1 - cur][:, sl], preferred_element_type=F32)
                m_old = m_ref[:, sl]
                parts = [jnp.where(tri, s_refs[cur][qh * half:(qh + 1) * half, sl], -jnp.inf)]
                if qh == 1:
                    parts.insert(0, s_refs[cur][0:half, sl])
                m_new = m_old
                for s in parts:
                    m_new = jnp.maximum(m_new, jnp.max(s, axis=0, keepdims=True))
                acc = jnp.exp2(m_old - m_new) * (acc_ref[:, sl] + pv)
                for kh, s in enumerate(parts):
                    acc = acc + jnp.dot(vt_own[:, kh * half:(kh + 1) * half], jnp.exp2(s - m_new).astype(BF16),
                                        preferred_element_type=F32)
                acc_ref[:, sl] = acc
                m_ref[:, sl] = m_new

    def sub_step(j, cur):
        scores(j + 1, 1 - cur)
        softmax_update(j, cur)

    scores(0, 0)

    def body(t, carry):
        sub_step(2 * t, 0)
        sub_step(2 * t + 1, 1)
        return carry

    lax.fori_loop(0, i // 2, body, 0)
    odd = i % 2 == 1

    @pl.when(odd)
    def _():
        sub_step(i - 1, 0)
        diagonal_update(1)

    @pl.when(jnp.logical_not(odd))
    def _():
        diagonal_update(0)

    lam = _lam_from(lamv_ref, lam_init)
    o_all = acc_ref[0:HEAD_W, :] / acc_ref[HEAD_W:HEAD_W + 1, :]
    ot = o_all[:, :bq] - lam * o_all[:, bq:]
    gain = jnp.concatenate([subln_ref[...]] * (bq // LANES), axis=1)
    ot = ot * lax.rsqrt(jnp.mean(ot * ot, axis=0, keepdims=True) + EPS) * gain * (1.0 - lam_init)
    o_ref[0] = ot.T.astype(o_ref.dtype)


def _attn_prompt(qt, k, vt, lamv, subln_col, lam_init):
    bsz, length, _ = k.shape
    bq = min(ATTN_BLOCK, length)
    assert length % bq == 0
    return pl.pallas_call(
        functools.partial(_attn_prompt_kernel, bq=bq, strip=min(ATTN_STRIP, bq), lam_init=lam_init),
        grid=(bsz, N_HEADS, length // bq),
        in_specs=[pl.BlockSpec((1, HEAD_W, bq), lambda b, h, i: (b, h, i)),
                  pl.BlockSpec((1, length, HEAD_W), lambda b, h, i: (b, 0, h)),
                  pl.BlockSpec((1, HEAD_W, length), lambda b, h, i: (b, h, 0)),
                  _const_spec(lamv.shape), _const_spec((HEAD_W, LANES))],
        out_specs=pl.BlockSpec((1, bq, HEAD_W), lambda b, h, i: (b, i, h)),
        out_shape=jax.ShapeDtypeStruct((bsz, length, D_MODEL), BF16),
        scratch_shapes=[pltpu.VMEM((1, 2 * bq), F32), pltpu.VMEM((HEAD_W + ONES_ROWS, 2 * bq), F32),
                        pltpu.VMEM((bq, 2 * bq), F32), pltpu.VMEM((bq, 2 * bq), F32),
                        pltpu.VMEM((bq, 2 * bq), BF16), pltpu.VMEM((bq, 2 * bq), BF16)],
        compiler_params=_params("parallel", "parallel", "arbitrary"),
        name="attn_prompt",
    )(qt, k, vt, lamv, subln_col)


def _attn_sample_kernel(pt_ref, q_ref, *refs, n_q, n_pages, lam_init):
    del pt_ref
    kc_refs = refs[:n_pages]
    vc_refs = refs[n_pages:2 * n_pages]
    kn_ref, vn_ref, lamv_ref, subln_ref, o_ref = refs[2 * n_pages:]
    grp = 2 * n_q
    q = q_ref[0].astype(F32)
    lane = lax.broadcasted_iota(jnp.int32, (n_q, HEAD_W), 1)
    r_q = lax.broadcasted_iota(jnp.int32, (grp, PAGE), 0) % n_q
    k_i = lax.broadcasted_iota(jnp.int32, (grp, PAGE), 1)
    new_mask = k_i <= r_q
    pad = jnp.zeros((PAGE - n_q, HEAD_W), BF16)
    lam = _lam_from(lamv_ref, lam_init)
    for h in range(N_HEADS):
        cols = slice(h * HEAD_W, (h + 1) * HEAD_W)
        qh = q[:, cols]
        qh2 = jnp.concatenate([jnp.where(lane < MAP_W, qh, 0.0), jnp.where(lane >= MAP_W, qh, 0.0)], axis=0).astype(BF16)
        s_pages = [jnp.dot(qh2, kc_refs[pg][0, h].astype(BF16), preferred_element_type=F32) for pg in range(n_pages)]
        k_new = jnp.concatenate([kn_ref[0, :, cols], pad], axis=0)
        s_new = jnp.where(new_mask, lax.dot_general(qh2, k_new, _NT, preferred_element_type=F32), -jnp.inf)
        s = jnp.concatenate(s_pages + [s_new], axis=1)
        m = jnp.max(s, axis=-1, keepdims=True)
        p = jnp.exp(s - m)
        l = jnp.sum(p, axis=-1, keepdims=True)
        pb = p.astype(BF16)
        v_new = jnp.concatenate([vn_ref[0, :, cols], pad], axis=0)
        acc = jnp.dot(pb[:, n_pages * PAGE:], v_new, preferred_element_type=F32)
        for pg in range(n_pages):
            v_h = vc_refs[pg][0, pl.ds(h, PAGE, stride=N_HEADS), :].astype(BF16)
            acc = acc + jnp.dot(pb[:, pg * PAGE:(pg + 1) * PAGE], v_h, preferred_element_type=F32)
        o_all = acc / l
        o = o_all[:n_q] - lam * o_all[n_q:]
        o_ref[0, :, cols] = (_rms(o, subln_ref[...]) * (1.0 - lam_init)).astype(o_ref.dtype)


def _attn_sample(q, cache_kt, cache_v2, page_table, k_new, v_new, lamv, subln, lam_init):
    bsz, n_q, _ = q.shape
    n_pages = page_table.shape[1]
    per_b = lambda b, pt: (b, 0, 0)
    k_specs = [pl.BlockSpec((1, N_HEADS, HEAD_W, PAGE), functools.partial(lambda b, pt, pg: (pt[b, pg], 0, 0, 0), pg=pg))
               for pg in range(n_pages)]
    v_specs = [pl.BlockSpec((1, PAGE * N_HEADS, HEAD_W), functools.partial(lambda b, pt, pg: (pt[b, pg], 0, 0), pg=pg))
               for pg in range(n_pages)]
    grid_spec = pltpu.PrefetchScalarGridSpec(
        num_scalar_prefetch=1,
        grid=(bsz,),
        in_specs=[pl.BlockSpec((1, n_q, D_MODEL), per_b)] + k_specs + v_specs
                 + [pl.BlockSpec((1, n_q, D_MODEL), per_b), pl.BlockSpec((1, n_q, D_MODEL), per_b),
                    pl.BlockSpec(lamv.shape, lambda b, pt: (0, 0)), pl.BlockSpec((1, HEAD_W), lambda b, pt: (0, 0))],
        out_specs=pl.BlockSpec((1, n_q, D_MODEL), per_b),
    )
    return pl.pallas_call(
        functools.partial(_attn_sample_kernel, n_q=n_q, n_pages=n_pages, lam_init=lam_init),
        grid_spec=grid_spec,
        out_shape=jax.ShapeDtypeStruct((bsz, n_q, D_MODEL), F32),
        compiler_params=_params("parallel"),
        name="attn_sample",
    )(page_table, q, *([cache_kt] * n_pages), *([cache_v2] * n_pages), k_new, v_new, lamv, subln)


def _rope_angles(pos):
    inv = ROPE_THETA ** (-jnp.arange(0, ROT_DIM, 2, dtype=F32) / ROT_DIM)
    ang = pos.astype(F32)[:, None] * inv[None, :]
    return jnp.cos(ang), jnp.sin(ang)


def _rope_tables(pos):
    half = ROT_DIM // 2
    cos, sin = _rope_angles(pos)
    n = pos.shape[0]
    ones = jnp.ones((n, MAP_W - ROT_DIM), F32)
    zeros = jnp.zeros((n, MAP_W - ROT_DIM), F32)
    z8 = jnp.zeros((n, half), F32)
    cos_g = jnp.concatenate([cos, cos, ones], axis=1)
    sa_g = jnp.concatenate([-sin, z8, zeros], axis=1)
    sb_g = jnp.concatenate([z8, sin, zeros], axis=1)
    rep = LANES // MAP_W
    return jnp.tile(cos_g, (1, rep)), jnp.tile(sa_g, (1, rep)), jnp.tile(sb_g, (1, rep))


def _pad_lanes(v):
    return jnp.zeros((1, LANES), F32).at[0, :v.shape[0]].set(v.astype(F32))


def _gain_col(g):
    return jnp.broadcast_to(jnp.tile(g.astype(F32), HEAD_W // MAP_W)[:, None], (HEAD_W, LANES))


def kernel(x_prompt, x_sample, cache_k, cache_v, page_table, state_delta, state_conv, ln_mix, ln_mlp, w_up, w_down,
           w_in_a, conv_a, a_log, dt_bias, o_norm_a, w_out_a, ln_kv, w_kv, k_norm, w_q_b, q_norm_b,
           lam_q1, lam_k1, lam_q2, lam_k2, subln_b, w_out_b):
    bsz, seq, _ = x_prompt.shape
    dbsz, dseq, _ = x_sample.shape
    n_pages = page_table.shape[1]
    past_len = n_pages * PAGE
    tp, ts = bsz * seq, dbsz * dseq
    row2 = lambda v: v.reshape(1, -1).astype(F32)

    w_in = w_in_a[0]
    w_main = w_in[:, :CONV_W + D_MODEL].astype(BF16)
    w_ba = jnp.zeros((D_MODEL, 2 * LANES), F32)
    w_ba = w_ba.at[:, :N_HEADS].set(w_in[:, CONV_W + D_MODEL:CONV_W + D_MODEL + N_HEADS])
    w_ba = w_ba.at[:, LANES:LANES + N_HEADS].set(w_in[:, CONV_W + D_MODEL + N_HEADS:]).astype(BF16)
    ln0 = row2(ln_mix[0])
    alog, dtb, onorm = _pad_lanes(a_log[0]), _pad_lanes(dt_bias[0]), row2(o_norm_a[0])

    xp2, xs2 = x_prompt.reshape(tp, D_MODEL), x_sample.reshape(ts, D_MODEL)
    qkv_p, z_p, ba_p = _proj_in(xp2, ln0, w_main, w_ba)
    qkv_s, z_s, ba_s = _proj_in(xs2, ln0, w_main, w_ba)

    nchunk_p = min(DELTA_STEP_CHUNKS, seq // DELTA_CHUNK)
    o_p, sd_p, ct_p = _delta(qkv_p.reshape(bsz, seq, CONV_W), z_p.reshape(bsz, seq, D_MODEL),
                             ba_p.reshape(bsz, seq, 2 * LANES), conv_a[0], alog, dtb, onorm,
                             chunk=DELTA_CHUNK, n_chunks=nchunk_p, bb=math.gcd(bsz, DELTA_STEP_SEQS))
    chunk_s = math.gcd(dseq, DELTA_CHUNK)
    o_s, sd_s, ct_s = _delta(qkv_s.reshape(dbsz, dseq, CONV_W), z_s.reshape(dbsz, dseq, D_MODEL),
                             ba_s.reshape(dbsz, dseq, 2 * LANES), conv_a[0], alog, dtb, onorm,
                             chunk=chunk_s, n_chunks=dseq // chunk_s, bb=math.gcd(dbsz, DELTA_STEP_SEQS_SAMPLED),
                             s0=state_delta[0], conv_buf=state_conv[0], out_dtype=F32)

    wo_a, wup0, wdn0 = w_out_a[0].astype(BF16), w_up[0].astype(BF16), w_down[0].astype(BF16)
    h_p = _post(xp2, o_p.reshape(tp, D_MODEL), wo_a, row2(ln_mlp[0]), wup0, wdn0)
    h_s = _post(xs2, o_s.reshape(ts, D_MODEL), wo_a, row2(ln_mlp[0]), wup0, wdn0)

    lam_init = 0.8 - 0.6 * math.exp(-0.3 * 1)
    wkv, wq = w_kv.astype(BF16), w_q_b[0].astype(BF16)
    ln1 = row2(ln_mix[1])
    cos_p, sin_p = _rope_angles(jnp.arange(seq, dtype=jnp.int32))
    kt_p, v_p, kb_p, vt_p, qt_p = _proj_kvq_t(h_p, bsz, row2(ln_kv), ln1, wkv.T, wkv[:, D_MODEL:], wq.T,
                                              _gain_col(k_norm), _gain_col(q_norm_b[0]), cos_p.T, sin_p.T)
    kn = jnp.tile(k_norm.astype(F32), LANES // MAP_W).reshape(1, LANES)
    qn = jnp.tile(q_norm_b[0].astype(F32), LANES // MAP_W).reshape(1, LANES)
    gi = jnp.arange(LANES) // MAP_W
    gmat = (gi[:, None] == gi[None, :]).astype(F32) / MAP_W
    pos_s = past_len + jnp.arange(dseq, dtype=jnp.int32)
    tile_s = min(ROW_TILE, ts) // dseq
    cos_s, sa_s, sb_s = (jnp.tile(t, (tile_s, 1)) for t in _rope_tables(pos_s))
    k_s, v_s, kb_s, vb_s, qb_s = _proj_kvq(h_s, row2(ln_kv), ln1, wkv, wq, kn, qn, gmat, cos_s, sa_s, sb_s)

    lamv = jnp.zeros((SUBLANES, LANES), F32)
    for r, vec in enumerate((lam_q1[0], lam_k1[0], lam_q2[0], lam_k2[0])):
        lamv = lamv.at[r, :MAP_W].set(vec.astype(F32))
    subln = row2(subln_b[0])
    subln_col = jnp.broadcast_to(subln_b[0].astype(F32)[:, None], (HEAD_W, LANES))
    n_pool = cache_k.shape[0]
    cache_kt = jnp.transpose(cache_k, (0, 2, 3, 4, 1)).reshape(n_pool, N_HEADS, HEAD_W, PAGE)
    cache_v2 = cache_v.reshape(n_pool, PAGE * N_HEADS, HEAD_W)
    a_p = _attn_prompt(qt_p, kb_p.reshape(bsz, seq, D_MODEL), vt_p, lamv, subln_col, lam_init)
    a_s = _attn_sample(qb_s.reshape(dbsz, dseq, D_MODEL), cache_kt, cache_v2, page_table,
                       kb_s.reshape(dbsz, dseq, D_MODEL), vb_s.reshape(dbsz, dseq, D_MODEL), lamv, subln, lam_init)

    wo_b, wup1, wdn1 = w_out_b[0].astype(BF16), w_up[1].astype(BF16), w_down[1].astype(BF16)
    y_p = _post(h_p, a_p.reshape(tp, D_MODEL), wo_b, row2(ln_mlp[1]), wup1, wdn1)
    y_s = _post(h_s, a_s.reshape(ts, D_MODEL), wo_b, row2(ln_mlp[1]), wup1, wdn1)

    k_prompt = jnp.transpose(kt_p.reshape(bsz, N_HEADS, 2, MAP_W, seq), (0, 4, 1, 2, 3))
    return (y_p.reshape(bsz, seq, D_MODEL), y_s.reshape(dbsz, dseq, D_MODEL),
            k_prompt, v_p.reshape(bsz, seq, N_HEADS, HEAD_W),
            k_s.reshape(dbsz, dseq, N_HEADS, 2, MAP_W), v_s.reshape(dbsz, dseq, N_HEADS, HEAD_W),
            sd_p[None], sd_s[None], ct_p[None], ct_s[None])
```

```python
import functools
import math

import jax
import jax.numpy as jnp
from jax import lax
from jax.experimental import pallas as pl
from jax.experimental.pallas import tpu as pltpu

F32 = jnp.float32
BF16 = jnp.bfloat16
EPS = 1e-6

D_MODEL = 1024
D_FF = 4 * D_MODEL
N_HEADS = 8
HEAD_W = 128
MAP_W = 64
CONV_TAPS = 4
CONV_W = 3 * D_MODEL
DELTA_CHUNK = 64
ROT_DIM = 16
ROPE_THETA = 500000.0
PAGE = 128
LANES = 128
SUBLANES = 8
ONES_ROWS = 16
LOG2E = math.log2(math.e)
VMEM_LIMIT = 56 * 1024 * 1024

ROW_TILE = 512
FF_CHUNK = 1024
ATTN_BLOCK = 1024
ATTN_STRIP = 512
DELTA_STEP_CHUNKS = 2
DELTA_STEP_SEQS = 2
DELTA_STEP_SEQS_SAMPLED = 8

_NT = (((1,), (1,)), ((), ()))
_HI = lax.Precision.HIGHEST


def _dot(a, b):
    return jnp.dot(a.astype(BF16), b.astype(BF16), preferred_element_type=F32)


def _dot_nt(a, b):
    return lax.dot_general(a.astype(BF16), b.astype(BF16), _NT, preferred_element_type=F32)


def _rms(x, g):
    return x * lax.rsqrt(jnp.mean(x * x, axis=-1, keepdims=True) + EPS) * g


def _sigmoid(x):
    return 1.0 / (1.0 + jnp.exp(-x))


def _softplus(x):
    return jnp.maximum(x, 0.0) + jnp.log1p(jnp.exp(-jnp.abs(x)))


def _const_spec(shape):
    nd = len(shape)
    return pl.BlockSpec(shape, lambda *_: (0,) * nd, pipeline_mode=pl.Buffered(1))


def _params(*sem):
    return pltpu.CompilerParams(dimension_semantics=sem, vmem_limit_bytes=VMEM_LIMIT)


def _row_tile(t, want):
    tm = min(t, want)
    assert t % tm == 0
    return tm


def _proj_in_kernel(x_ref, ln_ref, w_ref, wba_ref, qkv_ref, z_ref, ba_ref):
    h = _rms(x_ref[...], ln_ref[...]).astype(BF16)
    qkv_ref[...] = jnp.dot(h, w_ref[:, :CONV_W], preferred_element_type=F32)
    z_ref[...] = jnp.dot(h, w_ref[:, CONV_W:], preferred_element_type=F32)
    ba_ref[...] = jnp.dot(h, wba_ref[...], preferred_element_type=F32)


def _proj_in(x, ln, w_main, w_ba):
    t = x.shape[0]
    tm = _row_tile(t, ROW_TILE)
    row = lambda w: pl.BlockSpec((tm, w), lambda i: (i, 0))
    return pl.pallas_call(
        _proj_in_kernel,
        grid=(t // tm,),
        in_specs=[row(D_MODEL), _const_spec((1, D_MODEL)), _const_spec(w_main.shape), _const_spec(w_ba.shape)],
        out_specs=[row(CONV_W), row(D_MODEL), row(2 * LANES)],
        out_shape=[jax.ShapeDtypeStruct((t, CONV_W), F32), jax.ShapeDtypeStruct((t, D_MODEL), F32),
                   jax.ShapeDtypeStruct((t, 2 * LANES), F32)],
        compiler_params=_params("parallel"),
        name="proj_in",
    )(x, ln, w_main, w_ba)


def _delta_kernel(*refs, chunk, n_chunks, bb, has_init):
    if has_init:
        (qkv_ref, z_ref, ba_ref, s0_ref, cb_ref, cw_ref, alog_ref, dtb_ref, onorm_ref,
         o_ref, sfin_ref, ctail_ref,
         xp_ref, q_s, k_s, v_s, g_s, beta_s, s_ref, wq_s, ub_s, at_s, kd_s, ee_s) = refs
    else:
        (qkv_ref, z_ref, ba_ref, cw_ref, alog_ref, dtb_ref, onorm_ref,
         o_ref, sfin_ref, ctail_ref,
         xp_ref, q_s, k_s, v_s, g_s, beta_s, s_ref, wq_s, ub_s, at_s, kd_s, ee_s) = refs
    c = chunk
    blk = c * n_chunks
    i = pl.program_id(1)
    last = pl.num_programs(1) - 1
    halo = SUBLANES
    hist = CONV_TAPS - 1

    @pl.when(i == 0)
    def _():
        xp_ref[:, 0:halo, :] = jnp.zeros((bb, halo, CONV_W), F32)
        if has_init:
            s_ref[...] = s0_ref[...]
            xp_ref[:, halo - hist:halo, :] = cb_ref[...]
        else:
            s_ref[...] = jnp.zeros_like(s_ref)

    for bi in range(bb):
        xp_ref[bi, halo:halo + blk, :] = qkv_ref[bi]
        ba = ba_ref[bi]
        beta_s[bi] = _sigmoid(ba[:, :LANES])
        g_s[bi] = -jnp.exp(alog_ref[...]) * _softplus(ba[:, LANES:] + dtb_ref[...])

    def conv_chunk(ci):
        r0 = ci * c
        for bi in range(bb):
            for j in range(CONV_W // LANES):
                cols = slice(j * LANES, (j + 1) * LANES)
                xg = xp_ref[bi, r0:r0 + halo + c, cols]
                acc = xg * cw_ref[0:1, cols]
                for w in range(1, CONV_TAPS):
                    acc = pltpu.roll(acc, 1, axis=0) + xg * cw_ref[w:w + 1, cols]
                acc = acc[halo:, :]
                y = acc / (1.0 + jnp.exp2(acc * (-LOG2E)))
                if j < 2 * N_HEADS:
                    inv_norm = lax.rsqrt(jnp.sum(y * y, axis=-1, keepdims=True) + EPS)
                if j < N_HEADS:
                    q_s[bi, r0:r0 + c, cols] = y * (inv_norm * (HEAD_W ** -0.5))
                elif j < 2 * N_HEADS:
                    k_s[bi, r0:r0 + c, (j - N_HEADS) * LANES:(j - N_HEADS + 1) * LANES] = y * inv_norm
                else:
                    v_s[bi, r0:r0 + c, (j - 2 * N_HEADS) * LANES:(j - 2 * N_HEADS + 1) * LANES] = y

    row = lax.broadcasted_iota(jnp.int32, (c, c), 0)
    col = lax.broadcasted_iota(jnp.int32, (c, c), 1)
    incl = row >= col
    strict = row > col
    ltri = incl.astype(F32)
    eye_c = (row == col).astype(F32)
    onorm = onorm_ref[...]
    n_sq = int(math.log2(c)) - 1
    heads = range(N_HEADS)
    hcols = [slice(h * LANES, (h + 1) * LANES) for h in heads]

    def phase_a(groups):
        inst = [(gi, h) for gi in range(len(groups)) for h in heads]
        rows, cg, cg_t, beta_c, ecg, ekd, bec = [], [], [], [], [], [], []
        for bi, ci in groups:
            r0 = ci * c if isinstance(ci, int) else pl.multiple_of(ci * c, c)
            rs = pl.ds(r0, c)
            g_c = g_s[bi, rs, :]
            b_c = beta_s[bi, rs, :]
            cg_c = jnp.dot(ltri, g_c, preferred_element_type=F32, precision=_HI)
            cg_t.append(cg_c.T)
            cg_last = cg_c[c - 1:c, :]
            e_c = jnp.exp(cg_c)
            ee_s[bi, ci] = jnp.exp(cg_last)
            rows.append(rs)
            cg.append(cg_c)
            beta_c.append(b_c)
            ecg.append(e_c)
            ekd.append(jnp.exp(cg_last - cg_c))
            bec.append(b_c * e_c)
        q = [q_s[groups[gi][0], rows[gi], hcols[h]] for gi, h in inst]
        k = [k_s[groups[gi][0], rows[gi], hcols[h]] for gi, h in inst]
        v = [v_s[groups[gi][0], rows[gi], hcols[h]] for gi, h in inst]
        kb = [x.astype(BF16) for x in k]
        kk = [_dot_nt(x, x) for x in kb]
        qk = [_dot_nt(a, b) for a, b in zip(q, kb)]
        beta_h = [beta_c[gi][:, h:h + 1] for gi, h in inst]
        diff = [cg[gi][:, h:h + 1] - cg_t[gi][h:h + 1, :] for gi, h in inst]
        decay = [jnp.where(incl, jnp.exp(jnp.where(incl, d, 0.0)), 0.0) for d in diff]
        nmat = [-(jnp.where(strict, dc * x, 0.0) * b) for dc, x, b in zip(decay, kk, beta_h)]
        pmat = [eye_c + n for n in nmat]
        qmat = [_dot(n, n) for n in nmat]
        for it in range(n_sq):
            if it < n_sq - 1:
                pq = [_dot(jnp.concatenate([p, n], axis=0), n) for p, n in zip(pmat, qmat)]
                pmat = [p + x[:c] for p, x in zip(pmat, pq)]
                qmat = [x[c:] for x in pq]
            else:
                pmat = [p + _dot(p, n) for p, n in zip(pmat, qmat)]
        rhs = [jnp.concatenate([kx * bec[gi][:, h:h + 1], vx * b], axis=1)
               for (gi, h), kx, vx, b in zip(inst, k, v, beta_h)]
        sol = [_dot(p, r) for p, r in zip(pmat, rhs)]
        k_dec = [kx * ekd[gi][:, h:h + 1] for (gi, h), kx in zip(inst, k)]
        kd_t = [x.T for x in k_dec]
        for n, (gi, h) in enumerate(inst):
            bi, ci = groups[gi]
            q_dec = q[n] * ecg[gi][:, h:h + 1]
            wq_s[bi, ci, h] = jnp.concatenate([sol[n][:, :LANES], q_dec], axis=0).astype(BF16)
            ub_s[bi, ci, h] = sol[n][:, LANES:]
            at_s[bi, ci, h] = decay[n] * qk[n]
            kd_s[bi, ci, h] = kd_t[n]

    def phase_b(ci):
        inst = [(bi, h) for bi in range(bb) for h in heads]
        r0 = ci * c if isinstance(ci, int) else pl.multiple_of(ci * c, c)
        rs = pl.ds(r0, c)
        s_old = [s_ref[bi, h] for bi, h in inst]
        r = [jnp.dot(wq_s[bi, ci, h], s.astype(BF16), preferred_element_type=F32) for (bi, h), s in zip(inst, s_old)]
        u = [ub_s[bi, ci, h] - x[:c] for (bi, h), x in zip(inst, r)]
        ub16 = [x.astype(BF16) for x in u]
        o = [x[c:] + _dot(at_s[bi, ci, h], y) for (bi, h), x, y in zip(inst, r, ub16)]
        for (bi, h), s, y in zip(inst, s_old, ub16):
            s_ref[bi, h] = s * ee_s[bi, ci][:, h:h + 1] + _dot(kd_s[bi, ci, h], y)
        for (bi, h), x in zip(inst, o):
            zt = z_ref[bi, rs, hcols[h]]
            o_ref[bi, rs, hcols[h]] = (_rms(x, onorm) * (zt * _sigmoid(zt))).astype(o_ref.dtype)

    conv_chunk(0)
    for ci in range(n_chunks):
        if ci + 1 < n_chunks:
            conv_chunk(ci + 1)
        phase_a([(bi, ci) for bi in range(bb)])

    for bi in range(bb):
        xp_ref[bi, 0:halo, :] = xp_ref[bi, blk:blk + halo, :]

        @pl.when(i == last)
        def _():
            ctail_ref[bi] = xp_ref[bi, halo - hist:halo, :]

    if n_chunks == 1:
        phase_b(0)
    else:
        def b_body(ci, carry):
            phase_b(ci)
            return carry

        lax.fori_loop(0, n_chunks, b_body, 0)

    @pl.when(i == last)
    def _():
        sfin_ref[...] = s_ref[...]


def _delta(qkv, z, ba, conv_w, a_log, dt_bias, o_norm, *, chunk, n_chunks, bb=1,
           s0=None, conv_buf=None, out_dtype=BF16):
    bsz, length, _ = qkv.shape
    blk = chunk * n_chunks
    assert length % blk == 0 and blk >= SUBLANES and bsz % bb == 0
    has_init = s0 is not None
    seq = lambda w: pl.BlockSpec((bb, blk, w), lambda b, i: (b, i, 0))
    state = pl.BlockSpec((bb, N_HEADS, HEAD_W, HEAD_W), lambda b, i: (b, 0, 0, 0))
    ctail = pl.BlockSpec((bb, CONV_TAPS - 1, CONV_W), lambda b, i: (b, 0, 0))
    in_specs = [seq(CONV_W), seq(D_MODEL), seq(2 * LANES)]
    args = [qkv, z, ba]
    if has_init:
        in_specs += [state, ctail]
        args += [s0, conv_buf]
    in_specs += [_const_spec(conv_w.shape), _const_spec((1, LANES)), _const_spec((1, LANES)), _const_spec((1, HEAD_W))]
    args += [conv_w, a_log, dt_bias, o_norm]
    per = (bb, n_chunks, N_HEADS)
    return pl.pallas_call(
        functools.partial(_delta_kernel, chunk=chunk, n_chunks=n_chunks, bb=bb, has_init=has_init),
        grid=(bsz // bb, length // blk),
        in_specs=in_specs,
        out_specs=[seq(D_MODEL), state, ctail],
        out_shape=[jax.ShapeDtypeStruct((bsz, length, D_MODEL), out_dtype),
                   jax.ShapeDtypeStruct((bsz, N_HEADS, HEAD_W, HEAD_W), F32),
                   jax.ShapeDtypeStruct((bsz, CONV_TAPS - 1, CONV_W), F32)],
        scratch_shapes=[pltpu.VMEM((bb, blk + SUBLANES, CONV_W), F32),
                        pltpu.VMEM((bb, blk, D_MODEL), F32), pltpu.VMEM((bb, blk, D_MODEL), F32),
                        pltpu.VMEM((bb, blk, D_MODEL), F32),
                        pltpu.VMEM((bb, blk, LANES), F32), pltpu.VMEM((bb, blk, LANES), F32),
                        pltpu.VMEM((bb, N_HEADS, HEAD_W, HEAD_W), F32),
                        pltpu.VMEM(per + (2 * chunk, HEAD_W), BF16), pltpu.VMEM(per + (chunk, HEAD_W), F32),
                        pltpu.VMEM(per + (chunk, chunk), F32), pltpu.VMEM(per + (HEAD_W, chunk), F32),
                        pltpu.VMEM((bb, n_chunks, 1, LANES), F32)],
        compiler_params=_params("parallel", "arbitrary"),
        name="delta_rule",
    )(*args)


def _post_kernel(x_ref, o_ref, wo_ref, ln_ref, wup_ref, wdn_ref, y_ref, *, ff_chunk):
    h1 = x_ref[...] + jnp.dot(o_ref[...].astype(BF16), wo_ref[...], preferred_element_type=F32)
    n = _rms(h1, ln_ref[...]).astype(BF16)
    acc = h1
    for c0 in range(0, D_FF, ff_chunk):
        u = jnp.dot(n, wup_ref[:, c0:c0 + ff_chunk], preferred_element_type=F32)
        a = jnp.square(jnp.maximum(u, 0.0)).astype(BF16)
        acc = acc + jnp.dot(a, wdn_ref[c0:c0 + ff_chunk, :], preferred_element_type=F32)
    y_ref[...] = acc


def _post(x, o, w_out, ln, w_up, w_down):
    t = x.shape[0]
    tm = _row_tile(t, ROW_TILE)
    row = pl.BlockSpec((tm, D_MODEL), lambda i: (i, 0))
    return pl.pallas_call(
        functools.partial(_post_kernel, ff_chunk=FF_CHUNK),
        grid=(t // tm,),
        in_specs=[row, row, _const_spec(w_out.shape), _const_spec((1, D_MODEL)),
                  _const_spec(w_up.shape), _const_spec(w_down.shape)],
        out_specs=row,
        out_shape=jax.ShapeDtypeStruct((t, D_MODEL), F32),
        compiler_params=_params("parallel"),
        name="post_mlp",
    )(x, o, w_out, ln, w_up, w_down)


def _head_norm_rope(x, gain, gmat, cos, sin_a, sin_b):
    outs = []
    for j in range(D_MODEL // LANES):
        xt = x[:, j * LANES:(j + 1) * LANES]
        ms = jnp.dot(xt * xt, gmat, preferred_element_type=F32, precision=_HI)
        xn = xt * lax.rsqrt(ms + EPS) * gain
        outs.append(xn * cos + pltpu.roll(xn, LANES - ROT_DIM // 2, axis=1) * sin_a
                    + pltpu.roll(xn, ROT_DIM // 2, axis=1) * sin_b)
    return jnp.concatenate(outs, axis=1)


def _proj_kvq_kernel(h_ref, lnkv_ref, lnq_ref, wkv_ref, wq_ref, kn_ref, qn_ref, gmat_ref, cos_ref, sa_ref, sb_ref,
                     k_ref, v_ref, kb_ref, vb_ref, qb_ref):
    h = h_ref[...]
    cos, sin_a, sin_b, gmat = cos_ref[...], sa_ref[...], sb_ref[...], gmat_ref[...]
    nkv = _rms(h, lnkv_ref[...]).astype(BF16)
    k = jnp.dot(nkv, wkv_ref[:, :D_MODEL], preferred_element_type=F32)
    v = jnp.dot(nkv, wkv_ref[:, D_MODEL:], preferred_element_type=F32)
    k = _head_norm_rope(k, kn_ref[...], gmat, cos, sin_a, sin_b)
    k_ref[...] = k
    v_ref[...] = v
    kb_ref[...] = k.astype(BF16)
    vb_ref[...] = v.astype(BF16)
    nq = _rms(h, lnq_ref[...]).astype(BF16)
    q = jnp.dot(nq, wq_ref[...], preferred_element_type=F32)
    q = _head_norm_rope(q, qn_ref[...], gmat, cos, sin_a, sin_b)
    qb_ref[...] = (q * (MAP_W ** -0.5)).astype(BF16)


def _proj_kvq(h, ln_kv, ln_q, w_kv, w_q, k_norm, q_norm, gmat, cos, sin_a, sin_b):
    t = h.shape[0]
    tm = _row_tile(t, min(ROW_TILE, cos.shape[0]))
    n_tab = cos.shape[0] // tm
    row = pl.BlockSpec((tm, D_MODEL), lambda i: (i, 0))
    tab = pl.BlockSpec((tm, LANES), lambda i: (i % n_tab, 0))
    f32_out = jax.ShapeDtypeStruct((t, D_MODEL), F32)
    bf_out = jax.ShapeDtypeStruct((t, D_MODEL), BF16)
    return pl.pallas_call(
        _proj_kvq_kernel,
        grid=(t // tm,),
        in_specs=[row, _const_spec((1, D_MODEL)), _const_spec((1, D_MODEL)), _const_spec(w_kv.shape), _const_spec(w_q.shape),
                  _const_spec((1, LANES)), _const_spec((1, LANES)), _const_spec((LANES, LANES)), tab, tab, tab],
        out_specs=[row] * 5,
        out_shape=[f32_out, f32_out, bf_out, bf_out, bf_out],
        compiler_params=_params("parallel"),
        name="proj_kvq",
    )(h, ln_kv, ln_q, w_kv, w_q, k_norm, q_norm, gmat, cos, sin_a, sin_b)


def _norm_rope_t(x, gain, cos, sin):
    half = ROT_DIM // 2
    outs = []
    for g in range(HEAD_W // MAP_W):
        xg = x[g * MAP_W:(g + 1) * MAP_W, :]
        ms = jnp.mean(xg * xg, axis=0, keepdims=True)
        xn = xg * lax.rsqrt(ms + EPS) * gain[g * MAP_W:(g + 1) * MAP_W, :]
        x1, x2 = xn[0:half], xn[half:ROT_DIM]
        outs += [x1 * cos - x2 * sin, x2 * cos + x1 * sin, xn[ROT_DIM:]]
    return jnp.concatenate(outs, axis=0)


def _proj_kvq_t_kernel(h_ref, lnkv_ref, lnq_ref, wkvt_ref, wv_ref, wqt_ref, kn_ref, qn_ref, cos_ref, sin_ref,
                       kt_ref, v_ref, kb_ref, vt_ref, qt_ref):
    h = h_ref[...]
    tm = h.shape[0]
    hn = h * lax.rsqrt(jnp.mean(h * h, axis=-1, keepdims=True) + EPS)
    nkv = (hn * lnkv_ref[...]).astype(BF16)
    nq = (hn * lnq_ref[...]).astype(BF16)
    cos, sin = cos_ref[...], sin_ref[...]
    rep = tm // LANES
    kn = jnp.concatenate([kn_ref[...]] * rep, axis=1)
    qn = jnp.concatenate([qn_ref[...]] * rep, axis=1)
    kvt = lax.dot_general(wkvt_ref[...], nkv, _NT, preferred_element_type=F32)
    kt = jnp.concatenate([_norm_rope_t(kvt[hd * HEAD_W:(hd + 1) * HEAD_W, :], kn, cos, sin)
                          for hd in range(N_HEADS)], axis=0)
    kt_ref[0] = kt
    vt_ref[0] = kvt[D_MODEL:, :].astype(BF16)
    kb_ref[...] = kt.T.astype(BF16)
    v_ref[...] = kvt[D_MODEL:, :].T
    qt = lax.dot_general(wqt_ref[...], nq, _NT, preferred_element_type=F32)
    qt = jnp.concatenate([_norm_rope_t(qt[hd * HEAD_W:(hd + 1) * HEAD_W, :], qn, cos, sin)
                          for hd in range(N_HEADS)], axis=0)
    qt_ref[0] = (qt * (MAP_W ** -0.5 * LOG2E)).astype(BF16)


def _proj_kvq_t(h, bsz, ln_kv, ln_q, w_kvt, w_v, w_qt, kn_col, qn_col, cos_t, sin_t):
    t = h.shape[0]
    length = t // bsz
    tm = _row_tile(length, ROW_TILE)
    nl = length // tm
    row = pl.BlockSpec((tm, D_MODEL), lambda b, i: (b * nl + i, 0))
    colb = pl.BlockSpec((1, D_MODEL, tm), lambda b, i: (b, 0, i))
    tab = pl.BlockSpec((ROT_DIM // 2, tm), lambda b, i: (0, i))
    t_f32 = jax.ShapeDtypeStruct((bsz, D_MODEL, length), F32)
    t_bf = jax.ShapeDtypeStruct((bsz, D_MODEL, length), BF16)
    return pl.pallas_call(
        _proj_kvq_t_kernel,
        grid=(bsz, nl),
        in_specs=[row, _const_spec((1, D_MODEL)), _const_spec((1, D_MODEL)), _const_spec(w_kvt.shape),
                  _const_spec(w_v.shape), _const_spec(w_qt.shape), _const_spec((HEAD_W, LANES)),
                  _const_spec((HEAD_W, LANES)), tab, tab],
        out_specs=[colb, row, row, colb, colb],
        out_shape=[t_f32, jax.ShapeDtypeStruct((t, D_MODEL), F32), jax.ShapeDtypeStruct((t, D_MODEL), BF16), t_bf, t_bf],
        compiler_params=_params("parallel", "parallel"),
        name="proj_kvq_t",
    )(h, ln_kv, ln_q, w_kvt, w_v, w_qt, kn_col, qn_col, cos_t, sin_t)


def _lam_from(lamv_ref, lam_init):
    lv = lamv_ref[...]
    s1 = jnp.sum(lv[0:1, :] * lv[1:2, :], axis=-1, keepdims=True)
    s2 = jnp.sum(lv[2:3, :] * lv[3:4, :], axis=-1, keepdims=True)
    return jnp.exp(s1) - jnp.exp(s2) + lam_init


def _attn_prompt_kernel(qt_ref, k_ref, vt_ref, lamv_ref, subln_ref, o_ref, m_ref, acc_ref,
                        s_ref, s1_ref, p_ref, p1_ref, *, bq, strip, lam_init):
    i = pl.program_id(2)
    qt = qt_ref[0]
    row = lax.broadcasted_iota(jnp.int32, (HEAD_W, bq), 0)
    zero = jnp.zeros_like(qt)
    q2t = jnp.concatenate([jnp.where(row < MAP_W, qt, zero), jnp.where(row >= MAP_W, qt, zero)], axis=1)
    strips = [slice(t * strip, (t + 1) * strip) for t in range(2 * bq // strip)]
    m_ref[...] = jnp.full(m_ref.shape, -jnp.inf, F32)
    acc_ref[...] = jnp.zeros_like(acc_ref)
    s_refs, p_refs = (s_ref, s1_ref), (p_ref, p1_ref)
    p1_ref[...] = jnp.zeros(p1_ref.shape, BF16)
    ones = jnp.ones((ONES_ROWS, bq), BF16)

    def keys(j):
        return k_ref[0, pl.ds(pl.multiple_of(j * bq, bq), bq), :]

    def values_t(j):
        vt = vt_ref[0, :, pl.ds(pl.multiple_of(j * bq, bq), bq)]
        return jnp.concatenate([vt, ones], axis=0)

    def scores(j, slot):
        kb = keys(j)
        for sl in strips:
            s_refs[slot][:, sl] = jnp.dot(kb, q2t[:, sl], preferred_element_type=F32)

    def softmax_update(j, cur):
        vt_prev = values_t(jnp.maximum(j - 1, 0))
        for sl in strips:
            pv = jnp.dot(vt_prev, p_refs[1 - cur][:, sl], preferred_element_type=F32)
            s = s_refs[cur][:, sl]
            m_old = m_ref[:, sl]
            m_new = jnp.maximum(m_old, jnp.max(s, axis=0, keepdims=True))
            alpha = jnp.exp2(m_old - m_new)
            p_refs[cur][:, sl] = jnp.exp2(s - m_new).astype(BF16)
            acc_ref[:, sl] = alpha * (acc_ref[:, sl] + pv)
            m_ref[:, sl] = m_new

    def diagonal_update(cur):
        half = bq // 2
        vt_prev = values_t(jnp.maximum(i - 1, 0))
        vt_own = values_t(i)
        tri = (lax.broadcasted_iota(jnp.int32, (half, half), 0) <= lax.broadcasted_iota(jnp.int32, (half, half), 1))
        for mp in range(2):
            for qh in range(2):
                sl = slice(mp * bq + qh * half, mp * bq + (qh + 1) * half)
                pv = jnp.dot(vt_prev, p_refs[1 - cur][:, sl], preferred_element_type=F32)
                m_old = m_ref[:, sl]
                parts = [jnp.where(tri, s_refs[cur][qh * half:(qh + 1) * half, sl], -jnp.inf)]
                if qh == 1:
                    parts.insert(0, s_refs[cur][0:half, sl])
                m_new = m_old
                for s in parts:
                    m_new = jnp.maximum(m_new, jnp.max(s, axis=0, keepdims=True))
                acc = jnp.exp2(m_old - m_new) * (acc_ref[:, sl] + pv)
                for kh, s in enumerate(parts):
                    acc = acc + jnp.dot(vt_own[:, kh * half:(kh + 1) * half], jnp.exp2(s - m_new).astype(BF16),
                                        preferred_element_type=F32)
                acc_ref[:, sl] = acc
                m_ref[:, sl] = m_new

    def sub_step(j, cur):
        scores(j + 1, 1 - cur)
        softmax_update(j, cur)

    scores(0, 0)

    def body(t, carry):
        sub_step(2 * t, 0)
        sub_step(2 * t + 1, 1)
        return carry

    lax.fori_loop(0, i // 2, body, 0)
    odd = i % 2 == 1

    @pl.when(odd)
    def _():
        sub_step(i - 1, 0)
        diagonal_update(1)

    @pl.when(jnp.logical_not(odd))
    def _():
        diagonal_update(0)

    lam = _lam_from(lamv_ref, lam_init)
    o_all = acc_ref[0:HEAD_W, :] / acc_ref[HEAD_W:HEAD_W + 1, :]
    ot = o_all[:, :bq] - lam * o_all[:, bq:]
    gain = jnp.concatenate([subln_ref[...]] * (bq // LANES), axis=1)
    ot = ot * lax.rsqrt(jnp.mean(ot * ot, axis=0, keepdims=True) + EPS) * gain * (1.0 - lam_init)
    o_ref[0] = ot.T.astype(o_ref.dtype)


def _attn_prompt(qt, k, vt, lamv, subln_col, lam_init):
    bsz, length, _ = k.shape
    bq = min(ATTN_BLOCK, length)
    assert length % bq == 0
    return pl.pallas_call(
        functools.partial(_attn_prompt_kernel, bq=bq, strip=min(ATTN_STRIP, bq), lam_init=lam_init),
        grid=(bsz, N_HEADS, length // bq),
        in_specs=[pl.BlockSpec((1, HEAD_W, bq), lambda b, h, i: (b, h, i)),
                  pl.BlockSpec((1, length, HEAD_W), lambda b, h, i: (b, 0, h)),
                  pl.BlockSpec((1, HEAD_W, length), lambda b, h, i: (b, h, 0)),
                  _const_spec(lamv.shape), _const_spec((HEAD_W, LANES))],
        out_specs=pl.BlockSpec((1, bq, HEAD_W), lambda b, h, i: (b, i, h)),
        out_shape=jax.ShapeDtypeStruct((bsz, length, D_MODEL), BF16),
        scratch_shapes=[pltpu.VMEM((1, 2 * bq), F32), pltpu.VMEM((HEAD_W + ONES_ROWS, 2 * bq), F32),
                        pltpu.VMEM((bq, 2 * bq), F32), pltpu.VMEM((bq, 2 * bq), F32),
                        pltpu.VMEM((bq, 2 * bq), BF16), pltpu.VMEM((bq, 2 * bq), BF16)],
        compiler_params=_params("parallel", "parallel", "arbitrary"),
        name="attn_prompt",
    )(qt, k, vt, lamv, subln_col)


def _attn_sample_kernel(pt_ref, q_ref, *refs, n_q, n_pages, lam_init):
    del pt_ref
    kc_refs = refs[:n_pages]
    vc_refs = refs[n_pages:2 * n_pages]
    kn_ref, vn_ref, lamv_ref, subln_ref, o_ref = refs[2 * n_pages:]
    grp = 2 * n_q
    q = q_ref[0].astype(F32)
    lane = lax.broadcasted_iota(jnp.int32, (n_q, HEAD_W), 1)
    r_q = lax.broadcasted_iota(jnp.int32, (grp, PAGE), 0) % n_q
    k_i = lax.broadcasted_iota(jnp.int32, (grp, PAGE), 1)
    new_mask = k_i <= r_q
    pad = jnp.zeros((PAGE - n_q, HEAD_W), BF16)
    lam = _lam_from(lamv_ref, lam_init)
    for h in range(N_HEADS):
        cols = slice(h * HEAD_W, (h + 1) * HEAD_W)
        qh = q[:, cols]
        qh2 = jnp.concatenate([jnp.where(lane < MAP_W, qh, 0.0), jnp.where(lane >= MAP_W, qh, 0.0)], axis=0).astype(BF16)
        s_pages = [jnp.dot(qh2, kc_refs[pg][0, h].astype(BF16), preferred_element_type=F32) for pg in range(n_pages)]
        k_new = jnp.concatenate([kn_ref[0, :, cols], pad], axis=0)
        s_new = jnp.where(new_mask, lax.dot_general(qh2, k_new, _NT, preferred_element_type=F32), -jnp.inf)
        s = jnp.concatenate(s_pages + [s_new], axis=1)
        m = jnp.max(s, axis=-1, keepdims=True)
        p = jnp.exp(s - m)
        l = jnp.sum(p, axis=-1, keepdims=True)
        pb = p.astype(BF16)
        v_new = jnp.concatenate([vn_ref[0, :, cols], pad], axis=0)
        acc = jnp.dot(pb[:, n_pages * PAGE:], v_new, preferred_element_type=F32)
        for pg in range(n_pages):
            v_h = vc_refs[pg][0, pl.ds(h, PAGE, stride=N_HEADS), :].astype(BF16)
            acc = acc + jnp.dot(pb[:, pg * PAGE:(pg + 1) * PAGE], v_h, preferred_element_type=F32)
        o_all = acc / l
        o = o_all[:n_q] - lam * o_all[n_q:]
        o_ref[0, :, cols] = (_rms(o, subln_ref[...]) * (1.0 - lam_init)).astype(o_ref.dtype)


def _attn_sample(q, cache_kt, cache_v2, page_table, k_new, v_new, lamv, subln, lam_init):
    bsz, n_q, _ = q.shape
    n_pages = page_table.shape[1]
    per_b = lambda b, pt: (b, 0, 0)
    k_specs = [pl.BlockSpec((1, N_HEADS, HEAD_W, PAGE), functools.partial(lambda b, pt, pg: (pt[b, pg], 0, 0, 0), pg=pg))
               for pg in range(n_pages)]
    v_specs = [pl.BlockSpec((1, PAGE * N_HEADS, HEAD_W), functools.partial(lambda b, pt, pg: (pt[b, pg], 0, 0), pg=pg))
               for pg in range(n_pages)]
    grid_spec = pltpu.PrefetchScalarGridSpec(
        num_scalar_prefetch=1,
        grid=(bsz,),
        in_specs=[pl.BlockSpec((1, n_q, D_MODEL), per_b)] + k_specs + v_specs
                 + [pl.BlockSpec((1, n_q, D_MODEL), per_b), pl.BlockSpec((1, n_q, D_MODEL), per_b),
                    pl.BlockSpec(lamv.shape, lambda b, pt: (0, 0)), pl.BlockSpec((1, HEAD_W), lambda b, pt: (0, 0))],
        out_specs=pl.BlockSpec((1, n_q, D_MODEL), per_b),
    )
    return pl.pallas_call(
        functools.partial(_attn_sample_kernel, n_q=n_q, n_pages=n_pages, lam_init=lam_init),
        grid_spec=grid_spec,
        out_shape=jax.ShapeDtypeStruct((bsz, n_q, D_MODEL), F32),
        compiler_params=_params("parallel"),
        name="attn_sample",
    )(page_table, q, *([cache_kt] * n_pages), *([cache_v2] * n_pages), k_new, v_new, lamv, subln)


def _rope_angles(pos):
    inv = ROPE_THETA ** (-jnp.arange(0, ROT_DIM, 2, dtype=F32) / ROT_DIM)
    ang = pos.astype(F32)[:, None] * inv[None, :]
    return jnp.cos(ang), jnp.sin(ang)


def _rope_tables(pos):
    half = ROT_DIM // 2
    cos, sin = _rope_angles(pos)
    n = pos.shape[0]
    ones = jnp.ones((n, MAP_W - ROT_DIM), F32)
    zeros = jnp.zeros((n, MAP_W - ROT_DIM), F32)
    z8 = jnp.zeros((n, half), F32)
    cos_g = jnp.concatenate([cos, cos, ones], axis=1)
    sa_g = jnp.concatenate([-sin, z8, zeros], axis=1)
    sb_g = jnp.concatenate([z8, sin, zeros], axis=1)
    rep = LANES // MAP_W
    return jnp.tile(cos_g, (1, rep)), jnp.tile(sa_g, (1, rep)), jnp.tile(sb_g, (1, rep))


def _pad_lanes(v):
    return jnp.zeros((1, LANES), F32).at[0, :v.shape[0]].set(v.astype(F32))


def _gain_col(g):
    return jnp.broadcast_to(jnp.tile(g.astype(F32), HEAD_W // MAP_W)[:, None], (HEAD_W, LANES))


def kernel(x_prompt, x_sample, cache_k, cache_v, page_table, state_delta, state_conv, ln_mix, ln_mlp, w_up, w_down,
           w_in_a, conv_a, a_log, dt_bias, o_norm_a, w_out_a, ln_kv, w_kv, k_norm, w_q_b, q_norm_b,
           lam_q1, lam_k1, lam_q2, lam_k2, subln_b, w_out_b):
    bsz, seq, _ = x_prompt.shape
    dbsz, dseq, _ = x_sample.shape
    n_pages = page_table.shape[1]
    past_len = n_pages * PAGE
    tp, ts = bsz * seq, dbsz * dseq
    row2 = lambda v: v.reshape(1, -1).astype(F32)

    w_in = w_in_a[0]
    w_main = w_in[:, :CONV_W + D_MODEL].astype(BF16)
    w_ba = jnp.zeros((D_MODEL, 2 * LANES), F32)
    w_ba = w_ba.at[:, :N_HEADS].set(w_in[:, CONV_W + D_MODEL:CONV_W + D_MODEL + N_HEADS])
    w_ba = w_ba.at[:, LANES:LANES + N_HEADS].set(w_in[:, CONV_W + D_MODEL + N_HEADS:]).astype(BF16)
    ln0 = row2(ln_mix[0])
    alog, dtb, onorm = _pad_lanes(a_log[0]), _pad_lanes(dt_bias[0]), row2(o_norm_a[0])

    xp2, xs2 = x_prompt.reshape(tp, D_MODEL), x_sample.reshape(ts, D_MODEL)
    qkv_p, z_p, ba_p = _proj_in(xp2, ln0, w_main, w_ba)
    qkv_s, z_s, ba_s = _proj_in(xs2, ln0, w_main, w_ba)

    nchunk_p = min(DELTA_STEP_CHUNKS, seq // DELTA_CHUNK)
    o_p, sd_p, ct_p = _delta(qkv_p.reshape(bsz, seq, CONV_W), z_p.reshape(bsz, seq, D_MODEL),
                             ba_p.reshape(bsz, seq, 2 * LANES), conv_a[0], alog, dtb, onorm,
                             chunk=DELTA_CHUNK, n_chunks=nchunk_p, bb=math.gcd(bsz, DELTA_STEP_SEQS))
    chunk_s = math.gcd(dseq, DELTA_CHUNK)
    o_s, sd_s, ct_s = _delta(qkv_s.reshape(dbsz, dseq, CONV_W), z_s.reshape(dbsz, dseq, D_MODEL),
                             ba_s.reshape(dbsz, dseq, 2 * LANES), conv_a[0], alog, dtb, onorm,
                             chunk=chunk_s, n_chunks=dseq // chunk_s, bb=math.gcd(dbsz, DELTA_STEP_SEQS_SAMPLED),
                             s0=state_delta[0], conv_buf=state_conv[0], out_dtype=F32)

    wo_a, wup0, wdn0 = w_out_a[0].astype(BF16), w_up[0].astype(BF16), w_down[0].astype(BF16)
    h_p = _post(xp2, o_p.reshape(tp, D_MODEL), wo_a, row2(ln_mlp[0]), wup0, wdn0)
    h_s = _post(xs2, o_s.reshape(ts, D_MODEL), wo_a, row2(ln_mlp[0]), wup0, wdn0)

    lam_init = 0.8 - 0.6 * math.exp(-0.3 * 1)
    wkv, wq = w_kv.astype(BF16), w_q_b[0].astype(BF16)
    ln1 = row2(ln_mix[1])
    cos_p, sin_p = _rope_angles(jnp.arange(seq, dtype=jnp.int32))
    kt_p, v_p, kb_p, vt_p, qt_p = _proj_kvq_t(h_p, bsz, row2(ln_kv), ln1, wkv.T, wkv[:, D_MODEL:], wq.T,
                                              _gain_col(k_norm), _gain_col(q_norm_b[0]), cos_p.T, sin_p.T)
    kn = jnp.tile(k_norm.astype(F32), LANES // MAP_W).reshape(1, LANES)
    qn = jnp.tile(q_norm_b[0].astype(F32), LANES // MAP_W).reshape(1, LANES)
    gi = jnp.arange(LANES) // MAP_W
    gmat = (gi[:, None] == gi[None, :]).astype(F32) / MAP_W
    pos_s = past_len + jnp.arange(dseq, dtype=jnp.int32)
    tile_s = min(ROW_TILE, ts) // dseq
    cos_s, sa_s, sb_s = (jnp.tile(t, (tile_s, 1)) for t in _rope_tables(pos_s))
    k_s, v_s, kb_s, vb_s, qb_s = _proj_kvq(h_s, row2(ln_kv), ln1, wkv, wq, kn, qn, gmat, cos_s, sa_s, sb_s)

    lamv = jnp.zeros((SUBLANES, LANES), F32)
    for r, vec in enumerate((lam_q1[0], lam_k1[0], lam_q2[0], lam_k2[0])):
        lamv = lamv.at[r, :MAP_W].set(vec.astype(F32))
    subln = row2(subln_b[0])
    subln_col = jnp.broadcast_to(subln_b[0].astype(F32)[:, None], (HEAD_W, LANES))
    n_pool = cache_k.shape[0]
    cache_kt = jnp.transpose(cache_k, (0, 2, 3, 4, 1)).reshape(n_pool, N_HEADS, HEAD_W, PAGE)
    cache_v2 = cache_v.reshape(n_pool, PAGE * N_HEADS, HEAD_W)
    a_p = _attn_prompt(qt_p, kb_p.reshape(bsz, seq, D_MODEL), vt_p, lamv, subln_col, lam_init)
    a_s = _attn_sample(qb_s.reshape(dbsz, dseq, D_MODEL), cache_kt, cache_v2, page_table,
                       kb_s.reshape(dbsz, dseq, D_MODEL), vb_s.reshape(dbsz, dseq, D_MODEL), lamv, subln, lam_init)

    wo_b, wup1, wdn1 = w_out_b[0].astype(BF16), w_up[1].astype(BF16), w_down[1].astype(BF16)
    y_p = _post(h_p, a_p.reshape(tp, D_MODEL), wo_b, row2(ln_mlp[1]), wup1, wdn1)
    y_s = _post(h_s, a_s.reshape(ts, D_MODEL), wo_b, row2(ln_mlp[1]), wup1, wdn1)

    k_prompt = jnp.transpose(kt_p.reshape(bsz, N_HEADS, 2, MAP_W, seq), (0, 4, 1, 2, 3))
    return (y_p.reshape(bsz, seq, D_MODEL), y_s.reshape(dbsz, dseq, D_MODEL),
            k_prompt, v_p.reshape(bsz, seq, N_HEADS, HEAD_W),
            k_s.reshape(dbsz, dseq, N_HEADS, 2, MAP_W), v_s.reshape(dbsz, dseq, N_HEADS, HEAD_W),
            sd_p[None], sd_s[None], ct_p[None], ct_s[None])
```

```python
import functools
import math

import jax
import jax.numpy as jnp
from jax import lax
from jax.experimental import pallas as pl
from jax.experimental.pallas import tpu as pltpu

F32 = jnp.float32
BF16 = jnp.bfloat16
EPS = 1e-6

D_MODEL = 1024
D_FF = 4 * D_MODEL
N_HEADS = 8
HEAD_W = 128
MAP_W = 64
CONV_TAPS = 4
CONV_W = 3 * D_MODEL
DELTA_CHUNK = 64
ROT_DIM = 16
ROPE_THETA = 500000.0
PAGE = 128
LANES = 128
SUBLANES = 8
ONES_ROWS = 16
LOG2E = math.log2(math.e)
VMEM_LIMIT = 56 * 1024 * 1024

ROW_TILE = 512
FF_CHUNK = 1024
ATTN_BLOCK = 1024
ATTN_STRIP = 512
DELTA_STEP_CHUNKS = 2
DELTA_STEP_SEQS = 2
DELTA_STEP_SEQS_SAMPLED = 8

_NT = (((1,), (1,)), ((), ()))
_HI = lax.Precision.HIGHEST


def _dot(a, b):
    return jnp.dot(a.astype(BF16), b.astype(BF16), preferred_element_type=F32)


def _dot_nt(a, b):
    return lax.dot_general(a.astype(BF16), b.astype(BF16), _NT, preferred_element_type=F32)


def _rms(x, g):
    return x * lax.rsqrt(jnp.mean(x * x, axis=-1, keepdims=True) + EPS) * g


def _sigmoid(x):
    return 1.0 / (1.0 + jnp.exp(-x))


def _softplus(x):
    return jnp.maximum(x, 0.0) + jnp.log1p(jnp.exp(-jnp.abs(x)))


def _const_spec(shape):
    nd = len(shape)
    return pl.BlockSpec(shape, lambda *_: (0,) * nd, pipeline_mode=pl.Buffered(1))


def _params(*sem):
    return pltpu.CompilerParams(dimension_semantics=sem, vmem_limit_bytes=VMEM_LIMIT)


def _row_tile(t, want):
    tm = min(t, want)
    assert t % tm == 0
    return tm


def _proj_in_kernel(x_ref, ln_ref, w_ref, wba_ref, qkv_ref, z_ref, ba_ref):
    h = _rms(x_ref[...], ln_ref[...]).astype(BF16)
    qkv_ref[...] = jnp.dot(h, w_ref[:, :CONV_W], preferred_element_type=F32)
    z_ref[...] = jnp.dot(h, w_ref[:, CONV_W:], preferred_element_type=F32)
    ba_ref[...] = jnp.dot(h, wba_ref[...], preferred_element_type=F32)


def _proj_in(x, ln, w_main, w_ba):
    t = x.shape[0]
    tm = _row_tile(t, ROW_TILE)
    row = lambda w: pl.BlockSpec((tm, w), lambda i: (i, 0))
    return pl.pallas_call(
        _proj_in_kernel,
        grid=(t // tm,),
        in_specs=[row(D_MODEL), _const_spec((1, D_MODEL)), _const_spec(w_main.shape), _const_spec(w_ba.shape)],
        out_specs=[row(CONV_W), row(D_MODEL), row(2 * LANES)],
        out_shape=[jax.ShapeDtypeStruct((t, CONV_W), F32), jax.ShapeDtypeStruct((t, D_MODEL), F32),
                   jax.ShapeDtypeStruct((t, 2 * LANES), F32)],
        compiler_params=_params("parallel"),
        name="proj_in",
    )(x, ln, w_main, w_ba)


def _delta_kernel(*refs, chunk, n_chunks, bb, has_init):
    if has_init:
        (qkv_ref, z_ref, ba_ref, s0_ref, cb_ref, cw_ref, alog_ref, dtb_ref, onorm_ref,
         o_ref, sfin_ref, ctail_ref,
         xp_ref, q_s, k_s, v_s, g_s, beta_s, s_ref, wq_s, ub_s, at_s, kd_s, ee_s) = refs
    else:
        (qkv_ref, z_ref, ba_ref, cw_ref, alog_ref, dtb_ref, onorm_ref,
         o_ref, sfin_ref, ctail_ref,
         xp_ref, q_s, k_s, v_s, g_s, beta_s, s_ref, wq_s, ub_s, at_s, kd_s, ee_s) = refs
    c = chunk
    blk = c * n_chunks
    i = pl.program_id(1)
    last = pl.num_programs(1) - 1
    halo = SUBLANES
    hist = CONV_TAPS - 1

    @pl.when(i == 0)
    def _():
        xp_ref[:, 0:halo, :] = jnp.zeros((bb, halo, CONV_W), F32)
        if has_init:
            s_ref[...] = s0_ref[...]
            xp_ref[:, halo - hist:halo, :] = cb_ref[...]
        else:
            s_ref[...] = jnp.zeros_like(s_ref)

    for bi in range(bb):
        xp_ref[bi, halo:halo + blk, :] = qkv_ref[bi]
        ba = ba_ref[bi]
        beta_s[bi] = _sigmoid(ba[:, :LANES])
        g_s[bi] = -jnp.exp(alog_ref[...]) * _softplus(ba[:, LANES:] + dtb_ref[...])

    def conv_chunk(ci):
        r0 = ci * c
        for bi in range(bb):
            for j in range(CONV_W // LANES):
                cols = slice(j * LANES, (j + 1) * LANES)
                xg = xp_ref[bi, r0:r0 + halo + c, cols]
                acc = xg * cw_ref[0:1, cols]
                for w in range(1, CONV_TAPS):
                    acc = pltpu.roll(acc, 1, axis=0) + xg * cw_ref[w:w + 1, cols]
                acc = acc[halo:, :]
                y = acc / (1.0 + jnp.exp2(acc * (-LOG2E)))
                if j < 2 * N_HEADS:
                    inv_norm = lax.rsqrt(jnp.sum(y * y, axis=-1, keepdims=True) + EPS)
                if j < N_HEADS:
                    q_s[bi, r0:r0 + c, cols] = y * (inv_norm * (HEAD_W ** -0.5))
                elif j < 2 * N_HEADS:
                    k_s[bi, r0:r0 + c, (j - N_HEADS) * LANES:(j - N_HEADS + 1) * LANES] = y * inv_norm
                else:
                    v_s[bi, r0:r0 + c, (j - 2 * N_HEADS) * LANES:(j - 2 * N_HEADS + 1) * LANES] = y

    row = lax.broadcasted_iota(jnp.int32, (c, c), 0)
    col = lax.broadcasted_iota(jnp.int32, (c, c), 1)
    incl = row >= col
    strict = row > col
    ltri = incl.astype(F32)
    eye_c = (row == col).astype(F32)
    onorm = onorm_ref[...]
    n_sq = int(math.log2(c)) - 1
    heads = range(N_HEADS)
    hcols = [slice(h * LANES, (h + 1) * LANES) for h in heads]

    def phase_a(groups):
        inst = [(gi, h) for gi in range(len(groups)) for h in heads]
        rows, cg, cg_t, beta_c, ecg, ekd, bec = [], [], [], [], [], [], []
        for bi, ci in groups:
            r0 = ci * c if isinstance(ci, int) else pl.multiple_of(ci * c, c)
            rs = pl.ds(r0, c)
            g_c = g_s[bi, rs, :]
            b_c = beta_s[bi, rs, :]
            cg_c = jnp.dot(ltri, g_c, preferred_element_type=F32, precision=_HI)
            cg_t.append(cg_c.T)
            cg_last = cg_c[c - 1:c, :]
            e_c = jnp.exp(cg_c)
            ee_s[bi, ci] = jnp.exp(cg_last)
            rows.append(rs)
            cg.append(cg_c)
            beta_c.append(b_c)
            ecg.append(e_c)
            ekd.append(jnp.exp(cg_last - cg_c))
            bec.append(b_c * e_c)
        q = [q_s[groups[gi][0], rows[gi], hcols[h]] for gi, h in inst]
        k = [k_s[groups[gi][0], rows[gi], hcols[h]] for gi, h in inst]
        v = [v_s[groups[gi][0], rows[gi], hcols[h]] for gi, h in inst]
        kb = [x.astype(BF16) for x in k]
        kk = [_dot_nt(x, x) for x in kb]
        qk = [_dot_nt(a, b) for a, b in zip(q, kb)]
        beta_h = [beta_c[gi][:, h:h + 1] for gi, h in inst]
        diff = [cg[gi][:, h:h + 1] - cg_t[gi][h:h + 1, :] for gi, h in inst]
        decay = [jnp.where(incl, jnp.exp(jnp.where(incl, d, 0.0)), 0.0) for d in diff]
        nmat = [-(jnp.where(strict, dc * x, 0.0) * b) for dc, x, b in zip(decay, kk, beta_h)]
        pmat = [eye_c + n for n in nmat]
        qmat = [_dot(n, n) for n in nmat]
        for it in range(n_sq):
            if it < n_sq - 1:
                pq = [_dot(jnp.concatenate([p, n], axis=0), n) for p, n in zip(pmat, qmat)]
                pmat = [p + x[:c] for p, x in zip(pmat, pq)]
                qmat = [x[c:] for x in pq]
            else:
                pmat = [p + _dot(p, n) for p, n in zip(pmat, qmat)]
        rhs = [jnp.concatenate([kx * bec[gi][:, h:h + 1], vx * b], axis=1)
               for (gi, h), kx, vx, b in zip(inst, k, v, beta_h)]
        sol = [_dot(p, r) for p, r in zip(pmat, rhs)]
        k_dec = [kx * ekd[gi][:, h:h + 1] for (gi, h), kx in zip(inst, k)]
        kd_t = [x.T for x in k_dec]
        for n, (gi, h) in enumerate(inst):
            bi, ci = groups[gi]
            q_dec = q[n] * ecg[gi][:, h:h + 1]
            wq_s[bi, ci, h] = jnp.concatenate([sol[n][:, :LANES], q_dec], axis=0).astype(BF16)
            ub_s[bi, ci, h] = sol[n][:, LANES:]
            at_s[bi, ci, h] = decay[n] * qk[n]
            kd_s[bi, ci, h] = kd_t[n]

    def phase_b(ci):
        inst = [(bi, h) for bi in range(bb) for h in heads]
        r0 = ci * c if isinstance(ci, int) else pl.multiple_of(ci * c, c)
        rs = pl.ds(r0, c)
        s_old = [s_ref[bi, h] for bi, h in inst]
        r = [jnp.dot(wq_s[bi, ci, h], s.astype(BF16), preferred_element_type=F32) for (bi, h), s in zip(inst, s_old)]
        u = [ub_s[bi, ci, h] - x[:c] for (bi, h), x in zip(inst, r)]
        ub16 = [x.astype(BF16) for x in u]
        o = [x[c:] + _dot(at_s[bi, ci, h], y) for (bi, h), x, y in zip(inst, r, ub16)]
        for (bi, h), s, y in zip(inst, s_old, ub16):
            s_ref[bi, h] = s * ee_s[bi, ci][:, h:h + 1] + _dot(kd_s[bi, ci, h], y)
        for (bi, h), x in zip(inst, o):
            q_s[bi, rs, hcols[h]] = x

    def gate_out(ci):
        rs = slice(ci * c, (ci + 1) * c)
        for bi in range(bb):
            for h in heads:
                zt = z_ref[bi, rs, hcols[h]]
                o_ref[bi, rs, hcols[h]] = (_rms(q_s[bi, rs, hcols[h]], onorm) * (zt * _sigmoid(zt))).astype(o_ref.dtype)

    conv_chunk(0)
    for ci in range(n_chunks):
        if ci + 1 < n_chunks:
            conv_chunk(ci + 1)
        phase_a([(bi, ci) for bi in range(bb)])

    for bi in range(bb):
        xp_ref[bi, 0:halo, :] = xp_ref[bi, blk:blk + halo, :]

        @pl.when(i == last)
        def _():
            ctail_ref[bi] = xp_ref[bi, halo - hist:halo, :]

    for ci in range(n_chunks):
        phase_b(ci)
        if ci > 0:
            gate_out(ci - 1)
    gate_out(n_chunks - 1)

    @pl.when(i == last)
    def _():
        sfin_ref[...] = s_ref[...]


def _delta(qkv, z, ba, conv_w, a_log, dt_bias, o_norm, *, chunk, n_chunks, bb=1,
           s0=None, conv_buf=None, out_dtype=BF16):
    bsz, length, _ = qkv.shape
    blk = chunk * n_chunks
    assert length % blk == 0 and blk >= SUBLANES and bsz % bb == 0
    has_init = s0 is not None
    seq = lambda w: pl.BlockSpec((bb, blk, w), lambda b, i: (b, i, 0))
    state = pl.BlockSpec((bb, N_HEADS, HEAD_W, HEAD_W), lambda b, i: (b, 0, 0, 0))
    ctail = pl.BlockSpec((bb, CONV_TAPS - 1, CONV_W), lambda b, i: (b, 0, 0))
    in_specs = [seq(CONV_W), seq(D_MODEL), seq(2 * LANES)]
    args = [qkv, z, ba]
    if has_init:
        in_specs += [state, ctail]
        args += [s0, conv_buf]
    in_specs += [_const_spec(conv_w.shape), _const_spec((1, LANES)), _const_spec((1, LANES)), _const_spec((1, HEAD_W))]
    args += [conv_w, a_log, dt_bias, o_norm]
    per = (bb, n_chunks, N_HEADS)
    return pl.pallas_call(
        functools.partial(_delta_kernel, chunk=chunk, n_chunks=n_chunks, bb=bb, has_init=has_init),
        grid=(bsz // bb, length // blk),
        in_specs=in_specs,
        out_specs=[seq(D_MODEL), state, ctail],
        out_shape=[jax.ShapeDtypeStruct((bsz, length, D_MODEL), out_dtype),
                   jax.ShapeDtypeStruct((bsz, N_HEADS, HEAD_W, HEAD_W), F32),
                   jax.ShapeDtypeStruct((bsz, CONV_TAPS - 1, CONV_W), F32)],
        scratch_shapes=[pltpu.VMEM((bb, blk + SUBLANES, CONV_W), F32),
                        pltpu.VMEM((bb, blk, D_MODEL), F32), pltpu.VMEM((bb, blk, D_MODEL), F32),
                        pltpu.VMEM((bb, blk, D_MODEL), F32),
                        pltpu.VMEM((bb, blk, LANES), F32), pltpu.VMEM((bb, blk, LANES), F32),
                        pltpu.VMEM((bb, N_HEADS, HEAD_W, HEAD_W), F32),
                        pltpu.VMEM(per + (2 * chunk, HEAD_W), BF16), pltpu.VMEM(per + (chunk, HEAD_W), F32),
                        pltpu.VMEM(per + (chunk, chunk), F32), pltpu.VMEM(per + (HEAD_W, chunk), F32),
                        pltpu.VMEM((bb, n_chunks, 1, LANES), F32)],
        compiler_params=_params("parallel", "arbitrary"),
        name="delta_rule",
    )(*args)


def _post_kernel(x_ref, o_ref, wo_ref, ln_ref, wup_ref, wdn_ref, y_ref, *, ff_chunk):
    h1 = x_ref[...] + jnp.dot(o_ref[...].astype(BF16), wo_ref[...], preferred_element_type=F32)
    n = _rms(h1, ln_ref[...]).astype(BF16)
    acc = h1
    for c0 in range(0, D_FF, ff_chunk):
        u = jnp.dot(n, wup_ref[:, c0:c0 + ff_chunk], preferred_element_type=F32)
        a = jnp.square(jnp.maximum(u, 0.0)).astype(BF16)
        acc = acc + jnp.dot(a, wdn_ref[c0:c0 + ff_chunk, :], preferred_element_type=F32)
    y_ref[...] = acc


def _post(x, o, w_out, ln, w_up, w_down):
    t = x.shape[0]
    tm = _row_tile(t, ROW_TILE)
    row = pl.BlockSpec((tm, D_MODEL), lambda i: (i, 0))
    return pl.pallas_call(
        functools.partial(_post_kernel, ff_chunk=FF_CHUNK),
        grid=(t // tm,),
        in_specs=[row, row, _const_spec(w_out.shape), _const_spec((1, D_MODEL)),
                  _const_spec(w_up.shape), _const_spec(w_down.shape)],
        out_specs=row,
        out_shape=jax.ShapeDtypeStruct((t, D_MODEL), F32),
        compiler_params=_params("parallel"),
        name="post_mlp",
    )(x, o, w_out, ln, w_up, w_down)


def _head_norm_rope(x, gain, gmat, cos, sin_a, sin_b):
    outs = []
    for j in range(D_MODEL // LANES):
        xt = x[:, j * LANES:(j + 1) * LANES]
        ms = jnp.dot(xt * xt, gmat, preferred_element_type=F32, precision=_HI)
        xn = xt * lax.rsqrt(ms + EPS) * gain
        outs.append(xn * cos + pltpu.roll(xn, LANES - ROT_DIM // 2, axis=1) * sin_a
                    + pltpu.roll(xn, ROT_DIM // 2, axis=1) * sin_b)
    return jnp.concatenate(outs, axis=1)


def _proj_kvq_kernel(h_ref, lnkv_ref, lnq_ref, wkv_ref, wq_ref, kn_ref, qn_ref, gmat_ref, cos_ref, sa_ref, sb_ref,
                     k_ref, v_ref, kb_ref, vb_ref, qb_ref):
    h = h_ref[...]
    cos, sin_a, sin_b, gmat = cos_ref[...], sa_ref[...], sb_ref[...], gmat_ref[...]
    nkv = _rms(h, lnkv_ref[...]).astype(BF16)
    k = jnp.dot(nkv, wkv_ref[:, :D_MODEL], preferred_element_type=F32)
    v = jnp.dot(nkv, wkv_ref[:, D_MODEL:], preferred_element_type=F32)
    k = _head_norm_rope(k, kn_ref[...], gmat, cos, sin_a, sin_b)
    k_ref[...] = k
    v_ref[...] = v
    kb_ref[...] = k.astype(BF16)
    vb_ref[...] = v.astype(BF16)
    nq = _rms(h, lnq_ref[...]).astype(BF16)
    q = jnp.dot(nq, wq_ref[...], preferred_element_type=F32)
    q = _head_norm_rope(q, qn_ref[...], gmat, cos, sin_a, sin_b)
    qb_ref[...] = (q * (MAP_W ** -0.5)).astype(BF16)


def _proj_kvq(h, ln_kv, ln_q, w_kv, w_q, k_norm, q_norm, gmat, cos, sin_a, sin_b):
    t = h.shape[0]
    tm = _row_tile(t, min(ROW_TILE, cos.shape[0]))
    n_tab = cos.shape[0] // tm
    row = pl.BlockSpec((tm, D_MODEL), lambda i: (i, 0))
    tab = pl.BlockSpec((tm, LANES), lambda i: (i % n_tab, 0))
    f32_out = jax.ShapeDtypeStruct((t, D_MODEL), F32)
    bf_out = jax.ShapeDtypeStruct((t, D_MODEL), BF16)
    return pl.pallas_call(
        _proj_kvq_kernel,
        grid=(t // tm,),
        in_specs=[row, _const_spec((1, D_MODEL)), _const_spec((1, D_MODEL)), _const_spec(w_kv.shape), _const_spec(w_q.shape),
                  _const_spec((1, LANES)), _const_spec((1, LANES)), _const_spec((LANES, LANES)), tab, tab, tab],
        out_specs=[row] * 5,
        out_shape=[f32_out, f32_out, bf_out, bf_out, bf_out],
        compiler_params=_params("parallel"),
        name="proj_kvq",
    )(h, ln_kv, ln_q, w_kv, w_q, k_norm, q_norm, gmat, cos, sin_a, sin_b)


def _norm_rope_t(x, gain, cos, sin):
    half = ROT_DIM // 2
    outs = []
    for g in range(HEAD_W // MAP_W):
        xg = x[g * MAP_W:(g + 1) * MAP_W, :]
        ms = jnp.mean(xg * xg, axis=0, keepdims=True)
        xn = xg * lax.rsqrt(ms + EPS) * gain[g * MAP_W:(g + 1) * MAP_W, :]
        x1, x2 = xn[0:half], xn[half:ROT_DIM]
        outs += [x1 * cos - x2 * sin, x2 * cos + x1 * sin, xn[ROT_DIM:]]
    return jnp.concatenate(outs, axis=0)


def _proj_kvq_t_kernel(h_ref, lnkv_ref, lnq_ref, wkvt_ref, wv_ref, wqt_ref, kn_ref, qn_ref, cos_ref, sin_ref,
                       kt_ref, v_ref, kb_ref, vt_ref, qt_ref):
    h = h_ref[...]
    tm = h.shape[0]
    hn = h * lax.rsqrt(jnp.mean(h * h, axis=-1, keepdims=True) + EPS)
    nkv = (hn * lnkv_ref[...]).astype(BF16)
    nq = (hn * lnq_ref[...]).astype(BF16)
    cos, sin = cos_ref[...], sin_ref[...]
    rep = tm // LANES
    kn = jnp.concatenate([kn_ref[...]] * rep, axis=1)
    qn = jnp.concatenate([qn_ref[...]] * rep, axis=1)
    kvt = lax.dot_general(wkvt_ref[...], nkv, _NT, preferred_element_type=F32)
    kt = jnp.concatenate([_norm_rope_t(kvt[hd * HEAD_W:(hd + 1) * HEAD_W, :], kn, cos, sin)
                          for hd in range(N_HEADS)], axis=0)
    kt_ref[0] = kt
    vt_ref[0] = kvt[D_MODEL:, :].astype(BF16)
    kb_ref[...] = kt.T.astype(BF16)
    v_ref[...] = kvt[D_MODEL:, :].T
    qt = lax.dot_general(wqt_ref[...], nq, _NT, preferred_element_type=F32)
    qt = jnp.concatenate([_norm_rope_t(qt[hd * HEAD_W:(hd + 1) * HEAD_W, :], qn, cos, sin)
                          for hd in range(N_HEADS)], axis=0)
    qt_ref[0] = (qt * (MAP_W ** -0.5 * LOG2E)).astype(BF16)


def _proj_kvq_t(h, bsz, ln_kv, ln_q, w_kvt, w_v, w_qt, kn_col, qn_col, cos_t, sin_t):
    t = h.shape[0]
    length = t // bsz
    tm = _row_tile(length, ROW_TILE)
    nl = length // tm
    row = pl.BlockSpec((tm, D_MODEL), lambda b, i: (b * nl + i, 0))
    colb = pl.BlockSpec((1, D_MODEL, tm), lambda b, i: (b, 0, i))
    tab = pl.BlockSpec((ROT_DIM // 2, tm), lambda b, i: (0, i))
    t_f32 = jax.ShapeDtypeStruct((bsz, D_MODEL, length), F32)
    t_bf = jax.ShapeDtypeStruct((bsz, D_MODEL, length), BF16)
    return pl.pallas_call(
        _proj_kvq_t_kernel,
        grid=(bsz, nl),
        in_specs=[row, _const_spec((1, D_MODEL)), _const_spec((1, D_MODEL)), _const_spec(w_kvt.shape),
                  _const_spec(w_v.shape), _const_spec(w_qt.shape), _const_spec((HEAD_W, LANES)),
                  _const_spec((HEAD_W, LANES)), tab, tab],
        out_specs=[colb, row, row, colb, colb],
        out_shape=[t_f32, jax.ShapeDtypeStruct((t, D_MODEL), F32), jax.ShapeDtypeStruct((t, D_MODEL), BF16), t_bf, t_bf],
        compiler_params=_params("parallel", "parallel"),
        name="proj_kvq_t",
    )(h, ln_kv, ln_q, w_kvt, w_v, w_qt, kn_col, qn_col, cos_t, sin_t)


def _lam_from(lamv_ref, lam_init):
    lv = lamv_ref[...]
    s1 = jnp.sum(lv[0:1, :] * lv[1:2, :], axis=-1, keepdims=True)
    s2 = jnp.sum(lv[2:3, :] * lv[3:4, :], axis=-1, keepdims=True)
    return jnp.exp(s1) - jnp.exp(s2) + lam_init


def _attn_prompt_kernel(qt_ref, k_ref, vt_ref, lamv_ref, subln_ref, o_ref, m_ref, acc_ref,
                        s_ref, s1_ref, p_ref, p1_ref, *, bq, strip, lam_init):
    i = pl.program_id(2)
    qt = qt_ref[0]
    row = lax.broadcasted_iota(jnp.int32, (HEAD_W, bq), 0)
    zero = jnp.zeros_like(qt)
    q2t = jnp.concatenate([jnp.where(row < MAP_W, qt, zero), jnp.where(row >= MAP_W, qt, zero)], axis=1)
    strips = [slice(t * strip, (t + 1) * strip) for t in range(2 * bq // strip)]
    m_ref[...] = jnp.full(m_ref.shape, -jnp.inf, F32)
    acc_ref[...] = jnp.zeros_like(acc_ref)
    s_refs, p_refs = (s_ref, s1_ref), (p_ref, p1_ref)
    p1_ref[...] = jnp.zeros(p1_ref.shape, BF16)
    ones = jnp.ones((ONES_ROWS, bq), BF16)

    def keys(j):
        return k_ref[0, pl.ds(pl.multiple_of(j * bq, bq), bq), :]

    def values_t(j):
        vt = vt_ref[0, :, pl.ds(pl.multiple_of(j * bq, bq), bq)]
        return jnp.concatenate([vt, ones], axis=0)

    def scores(j, slot):
        kb = keys(j)
        for sl in strips:
            s_refs[slot][:, sl] = jnp.dot(kb, q2t[:, sl], preferred_element_type=F32)

    def softmax_update(j, cur):
        vt_prev = values_t(jnp.maximum(j - 1, 0))
        for sl in strips:
            pv = jnp.dot(vt_prev, p_refs[1 - cur][:, sl], preferred_element_type=F32)
            s = s_refs[cur][:, sl]
            m_old = m_ref[:, sl]
            m_new = jnp.maximum(m_old, jnp.max(s, axis=0, keepdims=True))
            alpha = jnp.exp2(m_old - m_new)
            p_refs[cur][:, sl] = jnp.exp2(s - m_new).astype(BF16)
            acc_ref[:, sl] = alpha * (acc_ref[:, sl] + pv)
            m_ref[:, sl] = m_new

    def diagonal_update(cur):
        half = bq // 2
        vt_prev = values_t(jnp.maximum(i - 1, 0))
        vt_own = values_t(i)
        tri = (lax.broadcasted_iota(jnp.int32, (half, half), 0) <= lax.broadcasted_iota(jnp.int32, (half, half), 1))
        for mp in range(2):
            for qh in range(2):
                sl = slice(mp * bq + qh * half, mp * bq + (qh + 1) * half)
                pv = jnp.dot(vt_prev, p_refs[1 - cur][:, sl], preferred_element_type=F32)
                m_old = m_ref[:, sl]
                parts = [jnp.where(tri, s_refs[cur][qh * half:(qh + 1) * half, sl], -jnp.inf)]
                if qh == 1:
                    parts.insert(0, s_refs[cur][0:half, sl])
                m_new = m_old
                for s in parts:
                    m_new = jnp.maximum(m_new, jnp.max(s, axis=0, keepdims=True))
                acc = jnp.exp2(m_old - m_new) * (acc_ref[:, sl] + pv)
                for kh, s in enumerate(parts):
                    acc = acc + jnp.dot(vt_own[:, kh * half:(kh + 1) * half], jnp.exp2(s - m_new).astype(BF16),
                                        preferred_element_type=F32)
                acc_ref[:, sl] = acc
                m_ref[:, sl] = m_new

    def sub_step(j, cur):
        scores(j + 1, 1 - cur)
        softmax_update(j, cur)

    scores(0, 0)

    def body(t, carry):
        sub_step(2 * t, 0)
        sub_step(2 * t + 1, 1)
        return carry

    lax.fori_loop(0, i // 2, body, 0)
    odd = i % 2 == 1

    @pl.when(odd)
    def _():
        sub_step(i - 1, 0)
        diagonal_update(1)

    @pl.when(jnp.logical_not(odd))
    def _():
        diagonal_update(0)

    lam = _lam_from(lamv_ref, lam_init)
    o_all = acc_ref[0:HEAD_W, :] / acc_ref[HEAD_W:HEAD_W + 1, :]
    ot = o_all[:, :bq] - lam * o_all[:, bq:]
    gain = jnp.concatenate([subln_ref[...]] * (bq // LANES), axis=1)
    ot = ot * lax.rsqrt(jnp.mean(ot * ot, axis=0, keepdims=True) + EPS) * gain * (1.0 - lam_init)
    o_ref[0] = ot.T.astype(o_ref.dtype)


def _attn_prompt(qt, k, vt, lamv, subln_col, lam_init):
    bsz, length, _ = k.shape
    bq = min(ATTN_BLOCK, length)
    assert length % bq == 0
    return pl.pallas_call(
        functools.partial(_attn_prompt_kernel, bq=bq, strip=min(ATTN_STRIP, bq), lam_init=lam_init),
        grid=(bsz, N_HEADS, length // bq),
        in_specs=[pl.BlockSpec((1, HEAD_W, bq), lambda b, h, i: (b, h, i)),
                  pl.BlockSpec((1, length, HEAD_W), lambda b, h, i: (b, 0, h)),
                  pl.BlockSpec((1, HEAD_W, length), lambda b, h, i: (b, h, 0)),
                  _const_spec(lamv.shape), _const_spec((HEAD_W, LANES))],
        out_specs=pl.BlockSpec((1, bq, HEAD_W), lambda b, h, i: (b, i, h)),
        out_shape=jax.ShapeDtypeStruct((bsz, length, D_MODEL), BF16),
        scratch_shapes=[pltpu.VMEM((1, 2 * bq), F32), pltpu.VMEM((HEAD_W + ONES_ROWS, 2 * bq), F32),
                        pltpu.VMEM((bq, 2 * bq), F32), pltpu.VMEM((bq, 2 * bq), F32),
                        pltpu.VMEM((bq, 2 * bq), BF16), pltpu.VMEM((bq, 2 * bq), BF16)],
        compiler_params=_params("parallel", "parallel", "arbitrary"),
        name="attn_prompt",
    )(qt, k, vt, lamv, subln_col)


def _attn_sample_kernel(pt_ref, q_ref, *refs, n_q, n_pages, lam_init):
    del pt_ref
    kc_refs = refs[:n_pages]
    vc_refs = refs[n_pages:2 * n_pages]
    kn_ref, vn_ref, lamv_ref, subln_ref, o_ref = refs[2 * n_pages:]
    grp = 2 * n_q
    q = q_ref[0].astype(F32)
    lane = lax.broadcasted_iota(jnp.int32, (n_q, HEAD_W), 1)
    r_q = lax.broadcasted_iota(jnp.int32, (grp, PAGE), 0) % n_q
    k_i = lax.broadcasted_iota(jnp.int32, (grp, PAGE), 1)
    new_mask = k_i <= r_q
    pad = jnp.zeros((PAGE - n_q, HEAD_W), BF16)
    lam = _lam_from(lamv_ref, lam_init)
    for h in range(N_HEADS):
        cols = slice(h * HEAD_W, (h + 1) * HEAD_W)
        qh = q[:, cols]
        qh2 = jnp.concatenate([jnp.where(lane < MAP_W, qh, 0.0), jnp.where(lane >= MAP_W, qh, 0.0)], axis=0).astype(BF16)
        s_pages = [jnp.dot(qh2, kc_refs[pg][0, h].astype(BF16), preferred_element_type=F32) for pg in range(n_pages)]
        k_new = jnp.concatenate([kn_ref[0, :, cols], pad], axis=0)
        s_new = jnp.where(new_mask, lax.dot_general(qh2, k_new, _NT, preferred_element_type=F32), -jnp.inf)
        s = jnp.concatenate(s_pages + [s_new], axis=1)
        m = jnp.max(s, axis=-1, keepdims=True)
        p = jnp.exp(s - m)
        l = jnp.sum(p, axis=-1, keepdims=True)
        pb = p.astype(BF16)
        v_new = jnp.concatenate([vn_ref[0, :, cols], pad], axis=0)
        acc = jnp.dot(pb[:, n_pages * PAGE:], v_new, preferred_element_type=F32)
        for pg in range(n_pages):
            v_h = vc_refs[pg][0, pl.ds(h, PAGE, stride=N_HEADS), :].astype(BF16)
            acc = acc + jnp.dot(pb[:, pg * PAGE:(pg + 1) * PAGE], v_h, preferred_element_type=F32)
        o_all = acc / l
        o = o_all[:n_q] - lam * o_all[n_q:]
        o_ref[0, :, cols] = (_rms(o, subln_ref[...]) * (1.0 - lam_init)).astype(o_ref.dtype)


def _attn_sample(q, cache_kt, cache_v2, page_table, k_new, v_new, lamv, subln, lam_init):
    bsz, n_q, _ = q.shape
    n_pages = page_table.shape[1]
    per_b = lambda b, pt: (b, 0, 0)
    k_specs = [pl.BlockSpec((1, N_HEADS, HEAD_W, PAGE), functools.partial(lambda b, pt, pg: (pt[b, pg], 0, 0, 0), pg=pg))
               for pg in range(n_pages)]
    v_specs = [pl.BlockSpec((1, PAGE * N_HEADS, HEAD_W), functools.partial(lambda b, pt, pg: (pt[b, pg], 0, 0), pg=pg))
               for pg in range(n_pages)]
    grid_spec = pltpu.PrefetchScalarGridSpec(
        num_scalar_prefetch=1,
        grid=(bsz,),
        in_specs=[pl.BlockSpec((1, n_q, D_MODEL), per_b)] + k_specs + v_specs
                 + [pl.BlockSpec((1, n_q, D_MODEL), per_b), pl.BlockSpec((1, n_q, D_MODEL), per_b),
                    pl.BlockSpec(lamv.shape, lambda b, pt: (0, 0)), pl.BlockSpec((1, HEAD_W), lambda b, pt: (0, 0))],
        out_specs=pl.BlockSpec((1, n_q, D_MODEL), per_b),
    )
    return pl.pallas_call(
        functools.partial(_attn_sample_kernel, n_q=n_q, n_pages=n_pages, lam_init=lam_init),
        grid_spec=grid_spec,
        out_shape=jax.ShapeDtypeStruct((bsz, n_q, D_MODEL), F32),
        compiler_params=_params("parallel"),
        name="attn_sample",
    )(page_table, q, *([cache_kt] * n_pages), *([cache_v2] * n_pages), k_new, v_new, lamv, subln)


def _rope_angles(pos):
    inv = ROPE_THETA ** (-jnp.arange(0, ROT_DIM, 2, dtype=F32) / ROT_DIM)
    ang = pos.astype(F32)[:, None] * inv[None, :]
    return jnp.cos(ang), jnp.sin(ang)


def _rope_tables(pos):
    half = ROT_DIM // 2
    cos, sin = _rope_angles(pos)
    n = pos.shape[0]
    ones = jnp.ones((n, MAP_W - ROT_DIM), F32)
    zeros = jnp.zeros((n, MAP_W - ROT_DIM), F32)
    z8 = jnp.zeros((n, half), F32)
    cos_g = jnp.concatenate([cos, cos, ones], axis=1)
    sa_g = jnp.concatenate([-sin, z8, zeros], axis=1)
    sb_g = jnp.concatenate([z8, sin, zeros], axis=1)
    rep = LANES // MAP_W
    return jnp.tile(cos_g, (1, rep)), jnp.tile(sa_g, (1, rep)), jnp.tile(sb_g, (1, rep))


def _pad_lanes(v):
    return jnp.zeros((1, LANES), F32).at[0, :v.shape[0]].set(v.astype(F32))


def _gain_col(g):
    return jnp.broadcast_to(jnp.tile(g.astype(F32), HEAD_W // MAP_W)[:, None], (HEAD_W, LANES))


def kernel(x_prompt, x_sample, cache_k, cache_v, page_table, state_delta, state_conv, ln_mix, ln_mlp, w_up, w_down,
           w_in_a, conv_a, a_log, dt_bias, o_norm_a, w_out_a, ln_kv, w_kv, k_norm, w_q_b, q_norm_b,
           lam_q1, lam_k1, lam_q2, lam_k2, subln_b, w_out_b):
    bsz, seq, _ = x_prompt.shape
    dbsz, dseq, _ = x_sample.shape
    n_pages = page_table.shape[1]
    past_len = n_pages * PAGE
    tp, ts = bsz * seq, dbsz * dseq
    row2 = lambda v: v.reshape(1, -1).astype(F32)

    w_in = w_in_a[0]
    w_main = w_in[:, :CONV_W + D_MODEL].astype(BF16)
    w_ba = jnp.zeros((D_MODEL, 2 * LANES), F32)
    w_ba = w_ba.at[:, :N_HEADS].set(w_in[:, CONV_W + D_MODEL:CONV_W + D_MODEL + N_HEADS])
    w_ba = w_ba.at[:, LANES:LANES + N_HEADS].set(w_in[:, CONV_W + D_MODEL + N_HEADS:]).astype(BF16)
    ln0 = row2(ln_mix[0])
    alog, dtb, onorm = _pad_lanes(a_log[0]), _pad_lanes(dt_bias[0]), row2(o_norm_a[0])

    xp2, xs2 = x_prompt.reshape(tp, D_MODEL), x_sample.reshape(ts, D_MODEL)
    qkv_p, z_p, ba_p = _proj_in(xp2, ln0, w_main, w_ba)
    qkv_s, z_s, ba_s = _proj_in(xs2, ln0, w_main, w_ba)

    nchunk_p = min(DELTA_STEP_CHUNKS, seq // DELTA_CHUNK)
    o_p, sd_p, ct_p = _delta(qkv_p.reshape(bsz, seq, CONV_W), z_p.reshape(bsz, seq, D_MODEL),
                             ba_p.reshape(bsz, seq, 2 * LANES), conv_a[0], alog, dtb, onorm,
                             chunk=DELTA_CHUNK, n_chunks=nchunk_p, bb=math.gcd(bsz, DELTA_STEP_SEQS))
    chunk_s = math.gcd(dseq, DELTA_CHUNK)
    o_s, sd_s, ct_s = _delta(qkv_s.reshape(dbsz, dseq, CONV_W), z_s.reshape(dbsz, dseq, D_MODEL),
                             ba_s.reshape(dbsz, dseq, 2 * LANES), conv_a[0], alog, dtb, onorm,
                             chunk=chunk_s, n_chunks=dseq // chunk_s, bb=math.gcd(dbsz, DELTA_STEP_SEQS_SAMPLED),
                             s0=state_delta[0], conv_buf=state_conv[0], out_dtype=F32)

    wo_a, wup0, wdn0 = w_out_a[0].astype(BF16), w_up[0].astype(BF16), w_down[0].astype(BF16)
    h_p = _post(xp2, o_p.reshape(tp, D_MODEL), wo_a, row2(ln_mlp[0]), wup0, wdn0)
    h_s = _post(xs2, o_s.reshape(ts, D_MODEL), wo_a, row2(ln_mlp[0]), wup0, wdn0)

    lam_init = 0.8 - 0.6 * math.exp(-0.3 * 1)
    wkv, wq = w_kv.astype(BF16), w_q_b[0].astype(BF16)
    ln1 = row2(ln_mix[1])
    cos_p, sin_p = _rope_angles(jnp.arange(seq, dtype=jnp.int32))
    kt_p, v_p, kb_p, vt_p, qt_p = _proj_kvq_t(h_p, bsz, row2(ln_kv), ln1, wkv.T, wkv[:, D_MODEL:], wq.T,
                                              _gain_col(k_norm), _gain_col(q_norm_b[0]), cos_p.T, sin_p.T)
    kn = jnp.tile(k_norm.astype(F32), LANES // MAP_W).reshape(1, LANES)
    qn = jnp.tile(q_norm_b[0].astype(F32), LANES // MAP_W).reshape(1, LANES)
    gi = jnp.arange(LANES) // MAP_W
    gmat = (gi[:, None] == gi[None, :]).astype(F32) / MAP_W
    pos_s = past_len + jnp.arange(dseq, dtype=jnp.int32)
    tile_s = min(ROW_TILE, ts) // dseq
    cos_s, sa_s, sb_s = (jnp.tile(t, (tile_s, 1)) for t in _rope_tables(pos_s))
    k_s, v_s, kb_s, vb_s, qb_s = _proj_kvq(h_s, row2(ln_kv), ln1, wkv, wq, kn, qn, gmat, cos_s, sa_s, sb_s)

    lamv = jnp.zeros((SUBLANES, LANES), F32)
    for r, vec in enumerate((lam_q1[0], lam_k1[0], lam_q2[0], lam_k2[0])):
        lamv = lamv.at[r, :MAP_W].set(vec.astype(F32))
    subln = row2(subln_b[0])
    subln_col = jnp.broadcast_to(subln_b[0].astype(F32)[:, None], (HEAD_W, LANES))
    n_pool = cache_k.shape[0]
    cache_kt = jnp.transpose(cache_k, (0, 2, 3, 4, 1)).reshape(n_pool, N_HEADS, HEAD_W, PAGE)
    cache_v2 = cache_v.reshape(n_pool, PAGE * N_HEADS, HEAD_W)
    a_p = _attn_prompt(qt_p, kb_p.reshape(bsz, seq, D_MODEL), vt_p, lamv, subln_col, lam_init)
    a_s = _attn_sample(qb_s.reshape(dbsz, dseq, D_MODEL), cache_kt, cache_v2, page_table,
                       kb_s.reshape(dbsz, dseq, D_MODEL), vb_s.reshape(dbsz, dseq, D_MODEL), lamv, subln, lam_init)

    wo_b, wup1, wdn1 = w_out_b[0].astype(BF16), w_up[1].astype(BF16), w_down[1].astype(BF16)
    y_p = _post(h_p, a_p.reshape(tp, D_MODEL), wo_b, row2(ln_mlp[1]), wup1, wdn1)
    y_s = _post(h_s, a_s.reshape(ts, D_MODEL), wo_b, row2(ln_mlp[1]), wup1, wdn1)

    k_prompt = jnp.transpose(kt_p.reshape(bsz, N_HEADS, 2, MAP_W, seq), (0, 4, 1, 2, 3))
    return (y_p.reshape(bsz, seq, D_MODEL), y_s.reshape(dbsz, dseq, D_MODEL),
            k_prompt, v_p.reshape(bsz, seq, N_HEADS, HEAD_W),
            k_s.reshape(dbsz, dseq, N_HEADS, 2, MAP_W), v_s.reshape(dbsz, dseq, N_HEADS, HEAD_W),
            sd_p[None], sd_s[None], ct_p[None], ct_s[None])
```
